```python
import jax, jax.numpy as jnp
from jax import lax
import numpy as np

D_MODEL = 1024
BATCH = 32
SEQ = 2048
DEPTH = 1

H_R = 8
N_R = 64
C_R = H_R * N_R
DECAY_LORA = 64
AAA_LORA = 64
GATE_LORA = 128
RW_SIZES = (C_R, C_R, C_R, DECAY_LORA, AAA_LORA, GATE_LORA)
RW_COLS = sum(RW_SIZES)
LNX_EPS = 64e-5

H_M = 4
DK_M = 64
DV_M = 128
C_MQK = H_M * DK_M
C_MV = H_M * DV_M
ML_SIZES = (C_MQK, C_MQK, C_MV, C_MV, H_M, H_M)
ML_COLS = sum(ML_SIZES)
QK_CONV = 4
CHUNK = 64
GATE_CAP = 15.0

N_IN = RW_COLS + ML_COLS
D_MIX = C_R + C_MV

D_FF = 2816
FFN_CONV = 3
NORM_EPS = 1e-6

kernel_name = "hymba_rwkv7_mlstm_convffn"


def _split(t, sizes):
    idx = np.cumsum(sizes)[:-1].tolist()
    return jnp.split(t, idx, axis=-1)


def rmsnorm(x, g):
    xf = x.astype(jnp.float32)
    y = xf * lax.rsqrt(jnp.mean(xf * xf, axis=-1, keepdims=True) + NORM_EPS)
    return y.astype(x.dtype) * g


def head_layernorm(y, eps):
    yf = y.astype(jnp.float32)
    mu = jnp.mean(yf, axis=-1, keepdims=True)
    var = jnp.mean(jnp.square(yf - mu), axis=-1, keepdims=True)
    return (yf - mu) * lax.rsqrt(var + eps)


def head_rmsnorm(y):
    yf = y.astype(jnp.float32)
    return yf * lax.rsqrt(jnp.mean(yf * yf, axis=-1, keepdims=True) + NORM_EPS)


def token_shift(t):
    return jnp.pad(t, ((0, 0), (1, 0), (0, 0)))[:, :-1]


def causal_dwconv(t, w, b):
    K = w.shape[0]
    T = t.shape[1]
    tp = jnp.pad(t, ((0, 0), (K - 1, 0), (0, 0)))
    out = b + tp[:, 0:T] * w[0]
    for j in range(1, K):
        out = out + tp[:, j:j + T] * w[j]
    return out


def softcap(t):
    return GATE_CAP * jnp.tanh(t / GATE_CAP)


def rwkv7_scan(r, w, k, v, a, b):
    Bn, T, H, N = r.shape

    def step(S, inp):
        r_t, w_t, k_t, v_t, a_t, b_t = inp
        sa = jnp.einsum('bhij,bhj->bhi', S, a_t)
        S = S * w_t[:, :, None, :] + sa[..., None] * b_t[:, :, None, :] + v_t[..., :, None] * k_t[..., None, :]
        y = jnp.einsum('bhij,bhj->bhi', S, r_t)
        return S, y

    S0 = jnp.zeros((Bn, H, N, N), jnp.float32)
    xs = tuple(jnp.moveaxis(t, 1, 0) for t in (r, w, k, v, a, b))
    _, y = lax.scan(step, S0, xs)
    return jnp.moveaxis(y, 0, 1)


def rwkv7_group(p, mu, w0, w_up_decay, a0, w_up_a, w_up_g, k_k, k_a, r_k, lnx_w, lnx_b):
    Bn, T, _ = p.shape
    p = p + (token_shift(p) - p) * mu
    r, k, v, w_lo, a_lo, g_lo = _split(p, RW_SIZES)
    log_w = -jax.nn.softplus(-(w0 + jnp.tanh(w_lo) @ w_up_decay)) - 0.5
    decay = jnp.exp(-jnp.exp(log_w.astype(jnp.float32)))
    a = jax.nn.sigmoid(a0 + a_lo @ w_up_a)
    g = jax.nn.sigmoid(g_lo) @ w_up_g
    heads = lambda t: t.reshape(Bn, T, H_R, N_R)
    kk = heads(k * k_k).astype(jnp.float32)
    kk = kk / jnp.maximum(jnp.sqrt(jnp.sum(kk * kk, axis=-1, keepdims=True)), 1e-12)
    k = k * (1 + (a - 1) * k_a)
    r_h, k_h, v_h, a_h = heads(r), heads(k), heads(v), heads(a)
    y = rwkv7_scan(r_h, heads(decay), k_h, v_h, -kk, kk * a_h)
    y = head_layernorm(y, LNX_EPS).reshape(Bn, T, C_R) * lnx_w + lnx_b
    bonus = jnp.sum(r_h * k_h * r_k, axis=-1, keepdims=True) * v_h
    return ((y + bonus.reshape(Bn, T, C_R)) * g).astype(p.dtype)


def mlstm_chunkwise(q, k, v, ig, logf):
    Bn, T, H, _ = q.shape
    nc = T // CHUNK

    def to_chunks(t):
        t = t.reshape(Bn, nc, CHUNK, H, *t.shape[3:])
        return jnp.moveaxis(t, (1, 3), (0, 2))

    causal = jnp.tril(jnp.ones((CHUNK, CHUNK), dtype=bool))

    def step(carry, inp):
        C, n, m = carry
        qc, kc, vc, ic, fc = inp
        b = jnp.cumsum(fc, axis=-1)
        log_d = jnp.where(causal, b[..., :, None] - b[..., None, :] + ic[..., None, :], -jnp.inf)
        m_inter = b + m[..., None]
        m_t = jnp.maximum(jnp.max(log_d, axis=-1), m_inter)
        d = jnp.exp(log_d - m_t[..., None])
        s = jnp.einsum('bhtd,bhsd->bhts', qc, kc) * d
        sc = jnp.exp(m_inter - m_t)
        num = jnp.einsum('bhts,bhsv->bhtv', s, vc) + sc[..., None] * jnp.einsum('bhtd,bhdv->bhtv', qc, C)
        den = jnp.sum(s, axis=-1) + sc * jnp.einsum('bhtd,bhd->bht', qc, n)
        h = num / jnp.maximum(jnp.abs(den), jnp.exp(-m_t))[..., None]
        b_last = b[..., -1]
        a_w = b_last[..., None] - b + ic
        m_new = jnp.maximum(b_last + m, jnp.max(a_w, axis=-1))
        w_s = jnp.exp(a_w - m_new[..., None])
        dec = jnp.exp(b_last + m - m_new)
        C = dec[..., None, None] * C + jnp.einsum('bhs,bhsd,bhsv->bhdv', w_s, kc, vc)
        n = dec[..., None] * n + jnp.einsum('bhs,bhsd->bhd', w_s, kc)
        return (C, n, m_new), h

    carry0 = (jnp.zeros((Bn, H, q.shape[-1], v.shape[-1]), jnp.float32),
              jnp.zeros((Bn, H, q.shape[-1]), jnp.float32),
              jnp.zeros((Bn, H), jnp.float32))
    xs = tuple(to_chunks(t) for t in (q, k, v, ig, logf))
    _, h = lax.scan(step, carry0, xs)
    h = jnp.moveaxis(h, (0, 2), (1, 3))
    return h.reshape(Bn, T, H, v.shape[-1])


def mlstm_group(p, qk_conv_w, qk_conv_b, i_bias, f_bias, mh_norm_g):
    Bn, T, _ = p.shape
    q, k, v, o, ig, fg = _split(p, ML_SIZES)
    qk = jax.nn.silu(causal_dwconv(jnp.concatenate([q, k], axis=-1), qk_conv_w, qk_conv_b))
    q, k = qk[..., :C_MQK], qk[..., C_MQK:]
    ig = softcap((ig + i_bias).astype(jnp.float32))
    logf = jax.nn.log_sigmoid(softcap((fg + f_bias).astype(jnp.float32)))
    h = mlstm_chunkwise(q.reshape(Bn, T, H_M, DK_M) * (DK_M ** -0.5),
                        k.reshape(Bn, T, H_M, DK_M),
                        v.reshape(Bn, T, H_M, DV_M), ig, logf)
    h = head_rmsnorm(h).reshape(Bn, T, C_MV) * mh_norm_g * jax.nn.sigmoid(o)
    return h.astype(p.dtype)


def setup_inputs(seed: int = 0) -> dict:
    key = jax.random.key(seed)
    ks = jax.random.split(key, 26)
    f32 = jnp.float32
    nrm = lambda k, s, sc: jax.random.normal(k, s, f32) * sc
    gain = lambda k, s: 1.0 + 0.02 * jax.random.normal(k, s, f32)
    L = DEPTH
    return {
        "x": jax.random.normal(ks[0], (BATCH, SEQ, D_MODEL), f32),
        "norm1_g": gain(ks[1], (L, D_MODEL)),
        "w_in": nrm(ks[2], (L, D_MODEL, N_IN), D_MODEL ** -0.5),
        "rw_mu": jax.random.uniform(ks[3], (L, RW_COLS), f32),
        "w0": jax.random.uniform(ks[4], (L, C_R), f32, -6.0, 0.0),
        "w_up_decay": nrm(ks[5], (L, DECAY_LORA, C_R), 0.1 * DECAY_LORA ** -0.5),
        "a0": nrm(ks[6], (L, C_R), 0.1),
        "w_up_a": nrm(ks[7], (L, AAA_LORA, C_R), 0.1 * AAA_LORA ** -0.5),
        "w_up_g": nrm(ks[8], (L, GATE_LORA, C_R), GATE_LORA ** -0.5),
        "k_k": 0.85 + 0.02 * jax.random.normal(ks[9], (L, C_R), f32),
        "k_a": gain(ks[10], (L, C_R)),
        "r_k": nrm(ks[11], (L, H_R, N_R), 0.1),
        "lnx_w": gain(ks[12], (L, C_R)),
        "lnx_b": nrm(ks[13], (L, C_R), 0.02),
        "qk_conv_w": nrm(ks[14], (L, QK_CONV, 2 * C_MQK), QK_CONV ** -0.5),
        "qk_conv_b": nrm(ks[15], (L, 2 * C_MQK), 0.02),
        "i_bias": nrm(ks[16], (L, H_M), 0.1),
        "f_bias": jax.random.uniform(ks[17], (L, H_M), f32, 3.0, 6.0),
        "mh_norm_g": gain(ks[18], (L, C_MV)),
        "w_out": nrm(ks[19], (L, D_MIX, D_MODEL), D_MIX ** -0.5),
        "norm2_g": gain(ks[20], (L, D_MODEL)),
        "w_ffn_up": nrm(ks[21], (L, D_MODEL, 2 * D_FF), D_MODEL ** -0.5),
        "ffn_conv_w": nrm(ks[22], (L, FFN_CONV, D_FF), FFN_CONV ** -0.5),
        "ffn_conv_b": nrm(ks[23], (L, D_FF), 0.02),
        "w_ffn_down": nrm(ks[24], (L, D_FF, D_MODEL), D_FF ** -0.5),
        "norm_f_g": gain(ks[25], (D_MODEL,)),
    }


def reference(x, norm1_g, w_in, rw_mu, w0, w_up_decay, a0, w_up_a, w_up_g, k_k, k_a, r_k,
              lnx_w, lnx_b, qk_conv_w, qk_conv_b, i_bias, f_bias, mh_norm_g, w_out,
              norm2_g, w_ffn_up, ffn_conv_w, ffn_conv_b, w_ffn_down, norm_f_g):
    for l in range(DEPTH):
        h = rmsnorm(x, norm1_g[l])
        proj = h @ w_in[l]
        p_rw, p_ml = proj[..., :RW_COLS], proj[..., RW_COLS:]
        y_rw = rwkv7_group(p_rw, rw_mu[l], w0[l], w_up_decay[l], a0[l], w_up_a[l], w_up_g[l],
                           k_k[l], k_a[l], r_k[l], lnx_w[l], lnx_b[l])
        y_ml = mlstm_group(p_ml, qk_conv_w[l], qk_conv_b[l], i_bias[l], f_bias[l],
                           mh_norm_g[l])
        x = x + jnp.concatenate([y_rw, y_ml], axis=-1) @ w_out[l]
        h = rmsnorm(x, norm2_g[l])
        u = h @ w_ffn_up[l]
        a, b = u[..., :D_FF], u[..., D_FF:]
        a = causal_dwconv(a, ffn_conv_w[l], ffn_conv_b[l])
        x = x + (jax.nn.silu(a) * b) @ w_ffn_down[l]
    return rmsnorm(x, norm_f_g)
```

```python
import functools
import math

import jax
import jax.numpy as jnp
from jax import lax
from jax.experimental import pallas as pl
from jax.experimental.pallas import tpu as pltpu

F32 = jnp.float32
BF16 = jnp.bfloat16

D_MODEL = 1024
H_R, N_R = 8, 64
C_R = H_R * N_R
DECAY_LORA, AAA_LORA, GATE_LORA = 64, 64, 128
RW_COLS = 3 * C_R + DECAY_LORA + AAA_LORA + GATE_LORA
LNX_EPS = 64e-5
H_M, DK_M, DV_M = 4, 64, 128
C_MQK = H_M * DK_M
C_MV = H_M * DV_M
ML_COLS = 2 * C_MQK + 2 * C_MV + 2 * H_M
QK_CONV = 4
GATE_CAP = 15.0
D_FF = 2816
FFN_CONV = 3
NORM_EPS = 1e-6
CHUNK = 64

LANES = 128
SUBLANES = 8
ML_PAD = 1664
GATE_OFF = 2 * C_MQK + 2 * C_MV
VMEM_LIMIT = 56 * 1024 * 1024

EXP_M05 = math.exp(-0.5)


def _dot(a, b):
    return jnp.dot(a.astype(BF16), b.astype(BF16), preferred_element_type=F32)


def _dot_nt(a, b):
    return lax.dot_general(a.astype(BF16), b.astype(BF16), (((1,), (1,)), ((), ())),
                           preferred_element_type=F32)


def _dot_tn(a, b):
    return lax.dot_general(a.astype(BF16), b.astype(BF16), (((0,), (0,)), ((), ())),
                           preferred_element_type=F32)


def _split_hi_lo(x):
    hi = x.astype(BF16)
    lo = (x - hi.astype(F32)).astype(BF16)
    return hi, lo


def _sigmoid(x):
    return 1.0 / (1.0 + jnp.exp(-x))


def _rmsnorm_rows(x, g):
    ms = jnp.mean(x * x, axis=-1, keepdims=True)
    return x * lax.rsqrt(ms + NORM_EPS) * g


def _inproj_kernel(x_ref, g_ref, w_ref, orw_ref, oml_ref):
    h = _rmsnorm_rows(x_ref[...], g_ref[...]).astype(BF16)
    step = 256
    for c in range(0, RW_COLS, step):
        orw_ref[:, c:c + step] = jnp.dot(h, w_ref[:, c:c + step], preferred_element_type=F32)
    for c in range(0, ML_PAD, step):
        n = min(step, ML_PAD - c)
        oml_ref[:, c:c + n] = jnp.dot(h, w_ref[:, RW_COLS + c:RW_COLS + c + n],
                                      preferred_element_type=F32)


def _inproj(x2d, g1, w_in_p, tm):
    n = x2d.shape[0]
    return pl.pallas_call(
        _inproj_kernel,
        grid=(n // tm,),
        in_specs=[
            pl.BlockSpec((tm, D_MODEL), lambda i: (i, 0)),
            pl.BlockSpec((1, D_MODEL), lambda i: (0, 0)),
            pl.BlockSpec((D_MODEL, RW_COLS + ML_PAD), lambda i: (0, 0)),
        ],
        out_specs=[
            pl.BlockSpec((tm, RW_COLS), lambda i: (i, 0)),
            pl.BlockSpec((tm, ML_PAD), lambda i: (i, 0)),
        ],
        out_shape=[
            jax.ShapeDtypeStruct((n, RW_COLS), F32),
            jax.ShapeDtypeStruct((n, ML_PAD), F32),
        ],
        compiler_params=pltpu.CompilerParams(
            dimension_semantics=("parallel",), vmem_limit_bytes=VMEM_LIMIT),
        name="inproj",
    )(x2d, g1, w_in_p)


def _rwkv_kernel(p_ref, mu_ref, vec_ref, wd_ref, wa_ref, wg_ref, ones_ref, o_ref,
                 carry_ref, state_ref, *, tb):
    t_idx = pl.program_id(1)

    @pl.when(t_idx == 0)
    def _():
        carry_ref[...] = jnp.zeros_like(carry_ref)
        state_ref[...] = jnp.zeros_like(state_ref)

    p = p_ref[...]
    prev_last = carry_ref[SUBLANES - 1:SUBLANES, :]
    rolled = pltpu.roll(p, 1, axis=0)
    row0 = lax.broadcasted_iota(jnp.int32, (tb, 1), 0) == 0
    shifted = jnp.where(row0, prev_last, rolled)
    carry_ref[...] = p[tb - SUBLANES:tb, :]
    pm = p + (shifted - p) * mu_ref[...]

    w0 = vec_ref[0:1, :]
    a0 = vec_ref[1:2, :]
    k_k = vec_ref[2:3, :]
    k_a = vec_ref[3:4, :]
    r_k = vec_ref[4:5, :]
    lnx_w = vec_ref[5:6, :]
    lnx_b = vec_ref[6:7, :]

    r = pm[:, 0:C_R]
    k = pm[:, C_R:2 * C_R]
    v = pm[:, 2 * C_R:3 * C_R]
    z = pm[:, 3 * C_R:3 * C_R + LANES]
    g_lo = pm[:, 3 * C_R + LANES:RW_COLS]

    ones_bd = ones_ref[...]
    u = w0 + _dot(jnp.tanh(z), wd_ref[...])
    ld = -EXP_M05 * _sigmoid(u)
    a = _sigmoid(a0 + _dot(z, wa_ref[...]))
    g = _dot(_sigmoid(g_lo), wg_ref[...])
    kk = k * k_k
    kkn = kk / jnp.maximum(jnp.sqrt(_dot(kk * kk, ones_bd)), 1e-12)
    kmod = k * (1.0 + (a - 1.0) * k_a)

    n2 = 2 * CHUNK
    ri = lax.broadcasted_iota(jnp.int32, (n2, n2), 0)
    ci = lax.broadcasted_iota(jnp.int32, (n2, n2), 1)
    strict_lower = ri > ci
    lower = ri >= ci
    eye = (ri == ci).astype(F32)
    lvl_masks = []
    for l in range(6):
        lvl_masks.append(((ri >> (l + 1)) == (ci >> (l + 1)))
                         & (((ri >> l) & 1) == 1) & (((ci >> l) & 1) == 0))
    lane_lo = lax.broadcasted_iota(jnp.int32, (CHUNK, LANES), 1) < N_R
    tri = (lax.broadcasted_iota(jnp.int32, (CHUNK, CHUNK), 0)
           >= lax.broadcasted_iota(jnp.int32, (CHUNK, CHUNK), 1)).astype(BF16)
    ones_cl = jnp.ones((CHUNK, LANES), BF16)

    def stack2(x):
        return jnp.concatenate([jnp.where(lane_lo, x, 0.0), jnp.where(lane_lo, 0.0, x)], axis=0)

    states = [state_ref[i] for i in range(H_R // 2)]
    y_rows = []
    for c in range(tb // CHUNK):
        sl = slice(c * CHUNK, (c + 1) * CHUNK)
        ld_c = ld[sl]
        hi, lo = _split_hi_lo(ld_c)
        cs = (jnp.dot(tri, hi, preferred_element_type=F32)
              + jnp.dot(tri, lo, preferred_element_type=F32))
        g_in = jnp.exp(cs)
        g_ex = jnp.exp(cs - ld_c)
        g_inv = jnp.exp(-cs)
        at = -kkn[sl] * g_ex
        rt = r[sl] * g_in
        bt = kkn[sl] * a[sl] * g_inv
        kt = kmod[sl] * g_inv
        v_c = v[sl]
        y_pairs = []
        for pr in range(H_R // 2):
            ls = slice(pr * LANES, (pr + 1) * LANES)
            m0 = states[pr]
            p_a, p_r, p_b, p_k, v_st = (stack2(t[:, ls]) for t in (at, rt, bt, kt, v_c))
            gram = _dot_nt(jnp.concatenate([p_a, p_r], axis=0),
                           jnp.concatenate([p_b, p_k], axis=0))
            g_ab = gram[0:n2, 0:n2]
            a_ak = jnp.where(strict_lower, gram[0:n2, n2:2 * n2], 0.0)
            a_rb = jnp.where(lower, gram[n2:2 * n2, 0:n2], 0.0)
            a_rk = jnp.where(lower, gram[n2:2 * n2, n2:2 * n2], 0.0)
            x = eye + jnp.where(lvl_masks[0], g_ab, 0.0)
            for l in range(1, 6):
                a_l = jnp.where(lvl_masks[l], g_ab, 0.0)
                x = x + _dot(_dot(x, a_l), x)
            w = _dot(jnp.concatenate([p_a, a_ak], axis=1), jnp.concatenate([m0, v_st], axis=0))
            u_st = _dot(x, w)
            y_st = _dot(jnp.concatenate([p_r, a_rb, a_rk], axis=1),
                        jnp.concatenate([m0, u_st, v_st], axis=0))
            y_pairs.append(y_st[0:CHUNK] + y_st[CHUNK:n2])
            hi_p, lo_p = _split_hi_lo(ld_c[:, ls])
            tot = (lax.dot_general(hi_p, ones_cl, (((0,), (0,)), ((), ())), preferred_element_type=F32)
                   + lax.dot_general(lo_p, ones_cl, (((0,), (0,)), ((), ())), preferred_element_type=F32))
            upd = _dot_tn(jnp.concatenate([p_b, p_k], axis=0), jnp.concatenate([u_st, v_st], axis=0))
            states[pr] = jnp.exp(tot) * (m0 + upd)
        y_rows.append(jnp.concatenate(y_pairs, axis=1))
    for pr in range(H_R // 2):
        state_ref[pr] = states[pr]
    y = jnp.concatenate(y_rows, axis=0)

    inv_n = 1.0 / N_R
    mean = _dot(y, ones_bd) * inv_n
    d = y - mean
    var = _dot(d * d, ones_bd) * inv_n
    yn = d * lax.rsqrt(var + LNX_EPS) * lnx_w + lnx_b
    bonus = _dot(r * kmod * r_k, ones_bd) * v
    o_ref[...] = ((yn + bonus) * g).astype(o_ref.dtype)


def _rwkv(p_rw, mu, vecs, wd, wa, wg, ones_bd, tb):
    bsz, t, _ = p_rw.shape
    const = lambda shape: pl.BlockSpec(shape, lambda b, i: (0,) * len(shape))
    return pl.pallas_call(
        functools.partial(_rwkv_kernel, tb=tb),
        grid=(bsz, t // tb),
        in_specs=[
            pl.BlockSpec((None, tb, RW_COLS), lambda b, i: (b, i, 0)),
            const((1, RW_COLS)),
            const((SUBLANES, C_R)),
            const((LANES, C_R)),
            const((LANES, C_R)),
            const((GATE_LORA, C_R)),
            const((C_R, C_R)),
        ],
        out_specs=pl.BlockSpec((None, tb, C_R), lambda b, i: (b, i, 0)),
        out_shape=jax.ShapeDtypeStruct((bsz, t, C_R), BF16),
        scratch_shapes=[
            pltpu.VMEM((SUBLANES, RW_COLS), F32),
            pltpu.VMEM((H_R // 2, 2 * N_R, 2 * N_R), F32),
        ],
        compiler_params=pltpu.CompilerParams(
            dimension_semantics=("parallel", "arbitrary"), vmem_limit_bytes=VMEM_LIMIT),
        name="rwkv7",
    )(p_rw, mu, vecs, wd, wa, wg, ones_bd)


def _mlstm_kernel(p_ref, cw_ref, cb_ref, gb_ref, ng_ref, o_ref,
                  carry_ref, c_ref, n_ref, m_ref, *, tb):
    t_idx = pl.program_id(1)

    @pl.when(t_idx == 0)
    def _():
        carry_ref[...] = jnp.zeros_like(carry_ref)
        c_ref[...] = jnp.zeros_like(c_ref)
        n_ref[...] = jnp.zeros_like(n_ref)
        m_ref[...] = jnp.zeros_like(m_ref)

    nqk = 2 * C_MQK
    qk_in = p_ref[:, 0:nqk]
    ext = jnp.concatenate([carry_ref[...], qk_in], axis=0)
    carry_ref[...] = qk_in[tb - SUBLANES:tb, :]
    acc = cb_ref[...] + qk_in * cw_ref[QK_CONV - 1:QK_CONV, :]
    for s in range(1, QK_CONV):
        sh = pltpu.roll(ext, s, axis=0)[SUBLANES:SUBLANES + tb]
        acc = acc + sh * cw_ref[QK_CONV - 1 - s:QK_CONV - s, :]
    qk = acc * _sigmoid(acc)
    q_all = qk[:, 0:C_MQK] * (DK_M ** -0.5)
    k_all = qk[:, C_MQK:nqk]
    v_all = p_ref[:, nqk:nqk + C_MV]
    o_gate = p_ref[:, nqk + C_MV:nqk + 2 * C_MV]

    graw = p_ref[:, GATE_OFF:GATE_OFF + LANES] + gb_ref[...]
    capped = GATE_CAP * jnp.tanh(graw / GATE_CAP)
    logsig = jnp.minimum(capped, 0.0) - jnp.log(1.0 + jnp.exp(-jnp.abs(capped)))
    lane_g = lax.broadcasted_iota(jnp.int32, (tb, LANES), 1)
    gates = jnp.where(lane_g < H_M, capped, logsig)

    n2 = 2 * CHUNK
    tri = (lax.broadcasted_iota(jnp.int32, (CHUNK, CHUNK), 0)
           >= lax.broadcasted_iota(jnp.int32, (CHUNK, CHUNK), 1)).astype(BF16)
    lane_lo = lax.broadcasted_iota(jnp.int32, (CHUNK, LANES), 1) < DK_M
    causal_st = ((lax.broadcasted_iota(jnp.int32, (n2, CHUNK), 0) & (CHUNK - 1))
                 >= lax.broadcasted_iota(jnp.int32, (n2, CHUNK), 1))
    zpad = jnp.zeros((CHUNK, LANES), F32)

    def stack2(x):
        return jnp.concatenate([jnp.where(lane_lo, x, 0.0), jnp.where(lane_lo, 0.0, x)], axis=0)

    def rows2(x0, x1, shape):
        return jnp.concatenate([jnp.broadcast_to(x0, shape), jnp.broadcast_to(x1, shape)], axis=0)

    c_st = [c_ref[i] for i in range(H_M // 2)]
    n_st = [n_ref[i] for i in range(H_M // 2)]
    m_st = [m_ref[i] for i in range(H_M // 2)]
    y_rows = []
    for c in range(tb // CHUNK):
        sl = slice(c * CHUNK, (c + 1) * CHUNK)
        g_c = gates[sl]
        hi, lo = _split_hi_lo(g_c)
        cs = (jnp.dot(tri, hi, preferred_element_type=F32)
              + jnp.dot(tri, lo, preferred_element_type=F32))
        g_t = jnp.concatenate([g_c, zpad], axis=0).T
        cs_t = jnp.concatenate([cs, zpad], axis=0).T
        y_heads = []
        for pr in range(H_M // 2):
            h0, h1 = 2 * pr, 2 * pr + 1
            ls = slice(pr * LANES, (pr + 1) * LANES)
            q_st = stack2(q_all[sl, ls])
            k_sl = k_all[sl, ls]
            s_qk = _dot_nt(q_st, k_sl)
            b_col = jnp.concatenate([cs[:, H_M + h0:H_M + h0 + 1], cs[:, H_M + h1:H_M + h1 + 1]], axis=0)
            i_col = jnp.concatenate([g_c[:, h0:h0 + 1], g_c[:, h1:h1 + 1]], axis=0)
            b_row = rows2(cs_t[H_M + h0:H_M + h0 + 1, 0:CHUNK], cs_t[H_M + h1:H_M + h1 + 1, 0:CHUNK],
                          (CHUNK, CHUNK))
            i_row = rows2(g_t[h0:h0 + 1, 0:CHUNK], g_t[h1:h1 + 1, 0:CHUNK], (CHUNK, CHUNK))
            m_prev = m_st[pr]
            log_d = jnp.where(causal_st, b_col - b_row + i_row, -jnp.inf)
            m_inter = b_col + m_prev
            m_t = jnp.maximum(jnp.max(log_d, axis=1, keepdims=True), m_inter)
            s_d = s_qk * jnp.exp(log_d - m_t)
            sc = jnp.exp(m_inter - m_t)
            v0 = v_all[sl, h0 * DV_M:(h0 + 1) * DV_M]
            v1 = v_all[sl, h1 * DV_M:(h1 + 1) * DV_M]
            sv = jnp.concatenate([_dot(s_d[0:CHUNK], v0), _dot(s_d[CHUNK:n2], v1)], axis=0)
            num = sv + sc * _dot(q_st, c_st[pr])
            den = (jnp.sum(s_d, axis=1, keepdims=True)
                   + sc * jnp.sum(q_st * n_st[pr], axis=1, keepdims=True))
            hout = num / jnp.maximum(jnp.abs(den), jnp.exp(-m_t))
            hn = hout * lax.rsqrt(jnp.mean(hout * hout, axis=1, keepdims=True) + NORM_EPS)
            y_heads.append(hn[0:CHUNK])
            y_heads.append(hn[CHUNK:n2])
            b_last = rows2(cs[CHUNK - 1:CHUNK, H_M + h0:H_M + h0 + 1],
                           cs[CHUNK - 1:CHUNK, H_M + h1:H_M + h1 + 1], (CHUNK, 1))
            a_w = b_last - b_col + i_col
            a_max = rows2(jnp.max(a_w[0:CHUNK], axis=0, keepdims=True),
                          jnp.max(a_w[CHUNK:n2], axis=0, keepdims=True), (CHUNK, 1))
            m_new = jnp.maximum(b_last + m_prev, a_max)
            w_s = jnp.exp(a_w - m_new)
            dec = jnp.exp(b_last + m_prev - m_new)
            kw = k_sl * jnp.where(lane_lo, w_s[0:CHUNK], w_s[CHUNK:n2])
            dec_row = jnp.where(lane_lo[0:1], dec[0:1], dec[CHUNK:CHUNK + 1])
            n_st[pr] = dec_row * n_st[pr] + jnp.sum(kw, axis=0, keepdims=True)
            row_lo = lax.broadcasted_iota(jnp.int32, (n2, LANES), 0) < CHUNK
            c_st[pr] = dec * c_st[pr] + jnp.where(row_lo, _dot_tn(kw, v0), _dot_tn(kw, v1))
            m_st[pr] = m_new
        y_rows.append(jnp.concatenate(y_heads, axis=1))
    for pr in range(H_M // 2):
        c_ref[pr] = c_st[pr]
        n_ref[pr] = n_st[pr]
        m_ref[pr] = m_st[pr]
    y = jnp.concatenate(y_rows, axis=0)
    o_ref[...] = (y * ng_ref[...] * _sigmoid(o_gate)).astype(o_ref.dtype)


def _mlstm(p_ml, cw, cb, gb, ng, tb):
    bsz, t, _ = p_ml.shape
    const = lambda shape: pl.BlockSpec(shape, lambda b, i: (0,) * len(shape))
    return pl.pallas_call(
        functools.partial(_mlstm_kernel, tb=tb),
        grid=(bsz, t // tb),
        in_specs=[
            pl.BlockSpec((None, tb, ML_PAD), lambda b, i: (b, i, 0)),
            const((QK_CONV, 2 * C_MQK)),
            const((1, 2 * C_MQK)),
            const((1, LANES)),
            const((1, C_MV)),
        ],
        out_specs=pl.BlockSpec((None, tb, C_MV), lambda b, i: (b, i, 0)),
        out_shape=jax.ShapeDtypeStruct((bsz, t, C_MV), BF16),
        scratch_shapes=[
            pltpu.VMEM((SUBLANES, 2 * C_MQK), F32),
            pltpu.VMEM((H_M // 2, 2 * DK_M, DV_M), F32),
            pltpu.VMEM((H_M // 2, 1, 2 * DK_M), F32),
            pltpu.VMEM((H_M // 2, 2 * CHUNK, 1), F32),
        ],
        compiler_params=pltpu.CompilerParams(
            dimension_semantics=("parallel", "arbitrary"), vmem_limit_bytes=VMEM_LIMIT),
        name="mlstm",
    )(p_ml, cw, cb, gb, ng)


def _ffn_kernel(x_ref, yr_ref, ym_ref, wo_ref, g2_ref, wup_ref, cw_ref, cb_ref, wdn_ref, gf_ref,
                o_ref, x1_ref, h2_ref, act_ref, carry_ref, *, tm, tiles_per_seq):
    i = pl.program_id(0)

    @pl.when(i % tiles_per_seq == 0)
    def _():
        carry_ref[...] = jnp.zeros_like(carry_ref)

    x1 = (x_ref[...]
          + jnp.dot(yr_ref[...], wo_ref[0:C_R, :], preferred_element_type=F32)
          + jnp.dot(ym_ref[...], wo_ref[C_R:C_R + C_MV, :], preferred_element_type=F32))
    x1_ref[...] = x1
    h2_ref[...] = _rmsnorm_rows(x1, g2_ref[...]).astype(BF16)

    fc = 256
    for j in range(D_FF // fc):
        cs = slice(j * fc, (j + 1) * fc)
        h2 = h2_ref[...]
        a = jnp.dot(h2, wup_ref[:, j * fc:(j + 1) * fc], preferred_element_type=F32)
        b = jnp.dot(h2, wup_ref[:, D_FF + j * fc:D_FF + (j + 1) * fc], preferred_element_type=F32)
        ext = jnp.concatenate([carry_ref[:, cs], a], axis=0)
        carry_ref[:, cs] = a[tm - SUBLANES:tm, :]
        conv = cb_ref[:, cs] + a * cw_ref[FFN_CONV - 1:FFN_CONV, cs]
        for s in range(1, FFN_CONV):
            sh = pltpu.roll(ext, s, axis=0)[SUBLANES:SUBLANES + tm]
            conv = conv + sh * cw_ref[FFN_CONV - 1 - s:FFN_CONV - s, cs]
        act_ref[:, cs] = (conv * _sigmoid(conv) * b).astype(BF16)

    nc = 256
    for j in range(D_MODEL // nc):
        cs = slice(j * nc, (j + 1) * nc)
        x1_ref[:, cs] = x1_ref[:, cs] + jnp.dot(act_ref[...], wdn_ref[:, cs],
                                                preferred_element_type=F32)
    o_ref[...] = _rmsnorm_rows(x1_ref[...], gf_ref[...])


def _ffn(x2d, y_rw, y_ml, w_out, g2, w_up, cw, cb, w_dn, gf, tm, seq):
    n = x2d.shape[0]
    const = lambda shape: pl.BlockSpec(shape, lambda i: (0,) * len(shape),
                                       pipeline_mode=pl.Buffered(1))
    return pl.pallas_call(
        functools.partial(_ffn_kernel, tm=tm, tiles_per_seq=seq // tm),
        grid=(n // tm,),
        in_specs=[
            pl.BlockSpec((tm, D_MODEL), lambda i: (i, 0)),
            pl.BlockSpec((tm, C_R), lambda i: (i, 0)),
            pl.BlockSpec((tm, C_MV), lambda i: (i, 0)),
            const((C_R + C_MV, D_MODEL)),
            const((1, D_MODEL)),
            const((D_MODEL, 2 * D_FF)),
            const((FFN_CONV, D_FF)),
            const((1, D_FF)),
            const((D_FF, D_MODEL)),
            const((1, D_MODEL)),
        ],
        out_specs=pl.BlockSpec((tm, D_MODEL), lambda i: (i, 0)),
        out_shape=jax.ShapeDtypeStruct((n, D_MODEL), F32),
        scratch_shapes=[
            pltpu.VMEM((tm, D_MODEL), F32),
            pltpu.VMEM((tm, D_MODEL), BF16),
            pltpu.VMEM((tm, D_FF), BF16),
            pltpu.VMEM((SUBLANES, D_FF), F32),
        ],
        compiler_params=pltpu.CompilerParams(
            dimension_semantics=("arbitrary",), vmem_limit_bytes=VMEM_LIMIT),
        name="outproj_convffn",
    )(x2d, y_rw, y_ml, w_out, g2, w_up, cw, cb, w_dn, gf)


def _row(v):
    return v.reshape(1, -1)


def _layer(x, norm1_g, w_in, rw_mu, w0, w_up_decay, a0, w_up_a, w_up_g, k_k, k_a, r_k,
           lnx_w, lnx_b, qk_conv_w, qk_conv_b, i_bias, f_bias, mh_norm_g, w_out,
           norm2_g, w_ffn_up, ffn_conv_w, ffn_conv_b, w_ffn_down, out_g):
    bsz, seq, _ = x.shape
    n = bsz * seq
    tm = min(512, seq)
    tb = min(256, seq)
    x2d = x.reshape(n, D_MODEL)

    w_in_p = jnp.pad(w_in, ((0, 0), (0, ML_PAD - ML_COLS))).astype(BF16)
    p_rw, p_ml = _inproj(x2d, _row(norm1_g), w_in_p, tm)

    zeros_lora = jnp.zeros((DECAY_LORA, C_R), F32)
    wd = jnp.concatenate([w_up_decay, zeros_lora], axis=0).astype(BF16)
    wa = jnp.concatenate([zeros_lora, w_up_a], axis=0).astype(BF16)
    vecs = jnp.stack([w0, a0, k_k, k_a, r_k.reshape(C_R), lnx_w, lnx_b, jnp.zeros_like(w0)], axis=0)
    head_id = jnp.arange(C_R) // N_R
    ones_bd = (head_id[:, None] == head_id[None, :]).astype(BF16)
    y_rw = _rwkv(p_rw.reshape(bsz, seq, RW_COLS), _row(rw_mu), vecs, wd, wa,
                 w_up_g.astype(BF16), ones_bd, tb)

    gate_bias = jnp.pad(jnp.concatenate([i_bias, f_bias]), (0, LANES - 2 * H_M))
    y_ml = _mlstm(p_ml.reshape(bsz, seq, ML_PAD), qk_conv_w, _row(qk_conv_b), _row(gate_bias),
                  _row(mh_norm_g), tb)

    out = _ffn(x2d, y_rw.reshape(n, C_R), y_ml.reshape(n, C_MV), w_out.astype(BF16), _row(norm2_g),
               w_ffn_up.astype(BF16), ffn_conv_w, _row(ffn_conv_b), w_ffn_down.astype(BF16),
               _row(out_g), tm, seq)
    return out.reshape(bsz, seq, D_MODEL)


def kernel(x, norm1_g, w_in, rw_mu, w0, w_up_decay, a0, w_up_a, w_up_g, k_k, k_a, r_k, lnx_w, lnx_b,
           qk_conv_w, qk_conv_b, i_bias, f_bias, mh_norm_g, w_out, norm2_g, w_ffn_up, ffn_conv_w,
           ffn_conv_b, w_ffn_down, norm_f_g):
    assert norm1_g.shape[0] == 1
    return _layer(x, norm1_g[0], w_in[0], rw_mu[0], w0[0], w_up_decay[0], a0[0], w_up_a[0],
                  w_up_g[0], k_k[0], k_a[0], r_k[0], lnx_w[0], lnx_b[0], qk_conv_w[0],
                  qk_conv_b[0], i_bias[0], f_bias[0], mh_norm_g[0], w_out[0], norm2_g[0],
                  w_ffn_up[0], ffn_conv_w[0], ffn_conv_b[0], w_ffn_down[0], norm_f_g)
```

```python
import functools
import math

import jax
import jax.numpy as jnp
from jax import lax
from jax.experimental import pallas as pl
from jax.experimental.pallas import tpu as pltpu

F32 = jnp.float32
BF16 = jnp.bfloat16

D_MODEL = 1024
H_R, N_R = 8, 64
C_R = H_R * N_R
DECAY_LORA, AAA_LORA, GATE_LORA = 64, 64, 128
RW_COLS = 3 * C_R + DECAY_LORA + AAA_LORA + GATE_LORA
LNX_EPS = 64e-5
H_M, DK_M, DV_M = 4, 64, 128
C_MQK = H_M * DK_M
C_MV = H_M * DV_M
ML_COLS = 2 * C_MQK + 2 * C_MV + 2 * H_M
QK_CONV = 4
GATE_CAP = 15.0
D_FF = 2816
FFN_CONV = 3
NORM_EPS = 1e-6
CHUNK = 64

LANES = 128
SUBLANES = 8
ML_PAD = 1664
GATE_OFF = 2 * C_MQK + 2 * C_MV
VMEM_LIMIT = 56 * 1024 * 1024

EXP_M05 = math.exp(-0.5)


def _dot(a, b):
    return jnp.dot(a.astype(BF16), b.astype(BF16), preferred_element_type=F32)


def _dot_nt(a, b):
    return lax.dot_general(a.astype(BF16), b.astype(BF16), (((1,), (1,)), ((), ())),
                           preferred_element_type=F32)


def _dot_tn(a, b):
    return lax.dot_general(a.astype(BF16), b.astype(BF16), (((0,), (0,)), ((), ())),
                           preferred_element_type=F32)


def _split_hi_lo(x):
    hi = x.astype(BF16)
    lo = (x - hi.astype(F32)).astype(BF16)
    return hi, lo


def _sigmoid(x):
    return 1.0 / (1.0 + jnp.exp(-x))


def _rmsnorm_rows(x, g):
    ms = jnp.mean(x * x, axis=-1, keepdims=True)
    return x * lax.rsqrt(ms + NORM_EPS) * g


def _inproj_kernel(x_ref, g_ref, w_ref, orw_ref, oml_ref):
    h = _rmsnorm_rows(x_ref[...], g_ref[...]).astype(BF16)
    step = 256
    for c in range(0, RW_COLS, step):
        orw_ref[:, c:c + step] = jnp.dot(h, w_ref[:, c:c + step], preferred_element_type=F32)
    for c in range(0, ML_PAD, step):
        n = min(step, ML_PAD - c)
        oml_ref[:, c:c + n] = jnp.dot(h, w_ref[:, RW_COLS + c:RW_COLS + c + n],
                                      preferred_element_type=F32)


def _inproj(x2d, g1, w_in_p, tm):
    n = x2d.shape[0]
    return pl.pallas_call(
        _inproj_kernel,
        grid=(n // tm,),
        in_specs=[
            pl.BlockSpec((tm, D_MODEL), lambda i: (i, 0)),
            pl.BlockSpec((1, D_MODEL), lambda i: (0, 0)),
            pl.BlockSpec((D_MODEL, RW_COLS + ML_PAD), lambda i: (0, 0)),
        ],
        out_specs=[
            pl.BlockSpec((tm, RW_COLS), lambda i: (i, 0)),
            pl.BlockSpec((tm, ML_PAD), lambda i: (i, 0)),
        ],
        out_shape=[
            jax.ShapeDtypeStruct((n, RW_COLS), F32),
            jax.ShapeDtypeStruct((n, ML_PAD), F32),
        ],
        compiler_params=pltpu.CompilerParams(
            dimension_semantics=("parallel",), vmem_limit_bytes=VMEM_LIMIT),
        name="inproj",
    )(x2d, g1, w_in_p)


def _rwkv_kernel(p_ref, mu_ref, vec_ref, wd_ref, wa_ref, wg_ref, ones_ref, o_ref,
                 carry_ref, state_ref, *, tb):
    t_idx = pl.program_id(1)

    @pl.when(t_idx == 0)
    def _():
        carry_ref[...] = jnp.zeros_like(carry_ref)
        state_ref[...] = jnp.zeros_like(state_ref)

    p = p_ref[...]
    prev_last = carry_ref[SUBLANES - 1:SUBLANES, :]
    rolled = pltpu.roll(p, 1, axis=0)
    row0 = lax.broadcasted_iota(jnp.int32, (tb, 1), 0) == 0
    shifted = jnp.where(row0, prev_last, rolled)
    carry_ref[...] = p[tb - SUBLANES:tb, :]
    pm = p + (shifted - p) * mu_ref[...]

    w0 = vec_ref[0:1, :]
    a0 = vec_ref[1:2, :]
    k_k = vec_ref[2:3, :]
    k_a = vec_ref[3:4, :]
    r_k = vec_ref[4:5, :]
    lnx_w = vec_ref[5:6, :]
    lnx_b = vec_ref[6:7, :]

    r = pm[:, 0:C_R]
    k = pm[:, C_R:2 * C_R]
    v = pm[:, 2 * C_R:3 * C_R]
    z = pm[:, 3 * C_R:3 * C_R + LANES]
    g_lo = pm[:, 3 * C_R + LANES:RW_COLS]

    ones_bd = ones_ref[...]
    u = w0 + _dot(jnp.tanh(z), wd_ref[...])
    ld = -EXP_M05 * _sigmoid(u)
    a = _sigmoid(a0 + _dot(z, wa_ref[...]))
    g = _dot(_sigmoid(g_lo), wg_ref[...])
    kk = k * k_k
    kkn = kk / jnp.maximum(jnp.sqrt(_dot(kk * kk, ones_bd)), 1e-12)
    kmod = k * (1.0 + (a - 1.0) * k_a)

    n2 = 2 * CHUNK
    ri = lax.broadcasted_iota(jnp.int32, (n2, n2), 0)
    ci = lax.broadcasted_iota(jnp.int32, (n2, n2), 1)
    strict_lower = ri > ci
    lower = ri >= ci
    eye = (ri == ci).astype(F32)
    lvl_masks = []
    for l in range(6):
        lvl_masks.append(((ri >> (l + 1)) == (ci >> (l + 1)))
                         & (((ri >> l) & 1) == 1) & (((ci >> l) & 1) == 0))
    lane_lo = lax.broadcasted_iota(jnp.int32, (CHUNK, LANES), 1) < N_R
    tri = (lax.broadcasted_iota(jnp.int32, (CHUNK, CHUNK), 0)
           >= lax.broadcasted_iota(jnp.int32, (CHUNK, CHUNK), 1)).astype(BF16)
    ones_cl = jnp.ones((CHUNK, LANES), BF16)

    def stack2(x):
        return jnp.concatenate([jnp.where(lane_lo, x, 0.0), jnp.where(lane_lo, 0.0, x)], axis=0)

    n_chunks = tb // CHUNK
    n_pairs = H_R // 2
    zeros_sq = jnp.zeros((n2, n2), BF16)

    prob = []
    for c in range(n_chunks):
        sl = slice(c * CHUNK, (c + 1) * CHUNK)
        ld_c = ld[sl]
        hi, lo = _split_hi_lo(ld_c)
        cs = (jnp.dot(tri, hi, preferred_element_type=F32)
              + jnp.dot(tri, lo, preferred_element_type=F32))
        g_in = jnp.exp(cs)
        g_ex = jnp.exp(cs - ld_c)
        g_inv = jnp.exp(-cs)
        at = -kkn[sl] * g_ex
        rt = r[sl] * g_in
        bt = kkn[sl] * a[sl] * g_inv
        kt = kmod[sl] * g_inv
        v_c = v[sl]
        for pr in range(n_pairs):
            ls = slice(pr * LANES, (pr + 1) * LANES)
            p_a, p_r, p_b, p_k, v_st = (stack2(t[:, ls]).astype(BF16) for t in (at, rt, bt, kt, v_c))
            p_bk = jnp.concatenate([p_b, p_k], axis=0)
            gram = lax.dot_general(jnp.concatenate([p_a, p_r], axis=0), p_bk,
                                   (((1,), (1,)), ((), ())), preferred_element_type=F32)
            g_ab = gram[0:n2, 0:n2]
            hi_p, lo_p = _split_hi_lo(ld_c[:, ls])
            tot = (lax.dot_general(hi_p, ones_cl, (((0,), (0,)), ((), ())), preferred_element_type=F32)
                   + lax.dot_general(lo_p, ones_cl, (((0,), (0,)), ((), ())), preferred_element_type=F32))
            prob.append(dict(
                p_a=p_a, p_r=p_r, p_bk=p_bk, v_st=v_st, g_ab=g_ab,
                a_ak=jnp.where(strict_lower, gram[0:n2, n2:2 * n2], 0.0).astype(BF16),
                a_rb=jnp.where(lower, gram[n2:2 * n2, 0:n2], 0.0).astype(BF16),
                a_rk=jnp.where(lower, gram[n2:2 * n2, n2:2 * n2], 0.0).astype(BF16),
                x=(eye + jnp.where(lvl_masks[0], g_ab, 0.0)).astype(BF16),
                g_last=jnp.exp(tot)))
    for l in range(1, 6):
        for q in prob:
            a_l = jnp.where(lvl_masks[l], q["g_ab"], 0.0).astype(BF16)
            q["t"] = jnp.dot(q["x"], a_l, preferred_element_type=F32).astype(BF16)
        for q in prob:
            q["x"] = q["x"] + jnp.dot(q["t"], q["x"], preferred_element_type=F32).astype(BF16)
    for q in prob:
        q["wv"] = jnp.dot(q["a_ak"], q["v_st"], preferred_element_type=F32).astype(BF16)
    for q in prob:
        xc = jnp.dot(q["x"], jnp.concatenate([q["wv"], q["p_a"]], axis=1), preferred_element_type=F32)
        q["xwv"] = xc[:, 0:n2].astype(BF16)
        q["xpa"] = xc[:, n2:2 * n2].astype(BF16)
    for q in prob:
        uv = jnp.concatenate([q["xwv"], q["v_st"]], axis=0)
        q["yv"] = jnp.dot(jnp.concatenate([q["a_rb"], q["a_rk"]], axis=1), uv,
                          preferred_element_type=F32)
        r_eff = q["p_r"].astype(F32) + jnp.dot(q["a_rb"], q["xpa"], preferred_element_type=F32)
        hb = lax.dot_general(q["p_bk"],
                             jnp.concatenate([uv, jnp.concatenate([q["xpa"], zeros_sq], axis=0)], axis=1),
                             (((0,), (0,)), ((), ())), preferred_element_type=F32)
        q["h"] = hb[:, 0:n2]
        q["lhs"] = jnp.concatenate([r_eff.astype(BF16), hb[:, n2:2 * n2].astype(BF16)], axis=0)

    states = [state_ref[i] for i in range(n_pairs)]
    y_rows = []
    for c in range(n_chunks):
        y_pairs = []
        for pr in range(n_pairs):
            q = prob[c * n_pairs + pr]
            m0 = states[pr]
            o2 = jnp.dot(q["lhs"], m0.astype(BF16), preferred_element_type=F32)
            y_st = o2[0:n2] + q["yv"]
            y_pairs.append(y_st[0:CHUNK] + y_st[CHUNK:n2])
            states[pr] = q["g_last"] * (m0 + o2[n2:2 * n2] + q["h"])
        y_rows.append(jnp.concatenate(y_pairs, axis=1))
    for pr in range(n_pairs):
        state_ref[pr] = states[pr]
    y = jnp.concatenate(y_rows, axis=0)

    inv_n = 1.0 / N_R
    mean = _dot(y, ones_bd) * inv_n
    d = y - mean
    var = _dot(d * d, ones_bd) * inv_n
    yn = d * lax.rsqrt(var + LNX_EPS) * lnx_w + lnx_b
    bonus = _dot(r * kmod * r_k, ones_bd) * v
    o_ref[...] = ((yn + bonus) * g).astype(o_ref.dtype)


def _rwkv(p_rw, mu, vecs, wd, wa, wg, ones_bd, tb):
    bsz, t, _ = p_rw.shape
    const = lambda shape: pl.BlockSpec(shape, lambda b, i: (0,) * len(shape))
    return pl.pallas_call(
        functools.partial(_rwkv_kernel, tb=tb),
        grid=(bsz, t // tb),
        in_specs=[
            pl.BlockSpec((None, tb, RW_COLS), lambda b, i: (b, i, 0)),
            const((1, RW_COLS)),
            const((SUBLANES, C_R)),
            const((LANES, C_R)),
            const((LANES, C_R)),
            const((GATE_LORA, C_R)),
            const((C_R, C_R)),
        ],
        out_specs=pl.BlockSpec((None, tb, C_R), lambda b, i: (b, i, 0)),
        out_shape=jax.ShapeDtypeStruct((bsz, t, C_R), BF16),
        scratch_shapes=[
            pltpu.VMEM((SUBLANES, RW_COLS), F32),
            pltpu.VMEM((H_R // 2, 2 * N_R, 2 * N_R), F32),
        ],
        compiler_params=pltpu.CompilerParams(
            dimension_semantics=("parallel", "arbitrary"), vmem_limit_bytes=VMEM_LIMIT),
        name="rwkv7",
    )(p_rw, mu, vecs, wd, wa, wg, ones_bd)


def _mlstm_kernel(p_ref, cw_ref, cb_ref, gb_ref, ng_ref, o_ref,
                  carry_ref, c_ref, n_ref, m_ref, *, tb):
    t_idx = pl.program_id(1)

    @pl.when(t_idx == 0)
    def _():
        carry_ref[...] = jnp.zeros_like(carry_ref)
        c_ref[...] = jnp.zeros_like(c_ref)
        n_ref[...] = jnp.zeros_like(n_ref)
        m_ref[...] = jnp.zeros_like(m_ref)

    nqk = 2 * C_MQK
    qk_in = p_ref[:, 0:nqk]
    ext = jnp.concatenate([carry_ref[...], qk_in], axis=0)
    carry_ref[...] = qk_in[tb - SUBLANES:tb, :]
    acc = cb_ref[...] + qk_in * cw_ref[QK_CONV - 1:QK_CONV, :]
    for s in range(1, QK_CONV):
        sh = pltpu.roll(ext, s, axis=0)[SUBLANES:SUBLANES + tb]
        acc = acc + sh * cw_ref[QK_CONV - 1 - s:QK_CONV - s, :]
    qk = acc * _sigmoid(acc)
    q_all = qk[:, 0:C_MQK] * (DK_M ** -0.5)
    k_all = qk[:, C_MQK:nqk]
    v_all = p_ref[:, nqk:nqk + C_MV]
    o_gate = p_ref[:, nqk + C_MV:nqk + 2 * C_MV]

    graw = p_ref[:, GATE_OFF:GATE_OFF + LANES] + gb_ref[...]
    capped = GATE_CAP * jnp.tanh(graw / GATE_CAP)
    logsig = jnp.minimum(capped, 0.0) - jnp.log(1.0 + jnp.exp(-jnp.abs(capped)))
    lane_g = lax.broadcasted_iota(jnp.int32, (tb, LANES), 1)
    gates = jnp.where(lane_g < H_M, capped, logsig)

    n2 = 2 * CHUNK
    tri = (lax.broadcasted_iota(jnp.int32, (CHUNK, CHUNK), 0)
           >= lax.broadcasted_iota(jnp.int32, (CHUNK, CHUNK), 1)).astype(BF16)
    lane_lo = lax.broadcasted_iota(jnp.int32, (CHUNK, LANES), 1) < DK_M
    causal_st = ((lax.broadcasted_iota(jnp.int32, (n2, CHUNK), 0) & (CHUNK - 1))
                 >= lax.broadcasted_iota(jnp.int32, (n2, CHUNK), 1))
    zpad = jnp.zeros((CHUNK, LANES), F32)

    def stack2(x):
        return jnp.concatenate([jnp.where(lane_lo, x, 0.0), jnp.where(lane_lo, 0.0, x)], axis=0)

    def rows2(x0, x1, shape):
        return jnp.concatenate([jnp.broadcast_to(x0, shape), jnp.broadcast_to(x1, shape)], axis=0)

    c_st = [c_ref[i] for i in range(H_M // 2)]
    n_st = [n_ref[i] for i in range(H_M // 2)]
    m_st = [m_ref[i] for i in range(H_M // 2)]
    y_rows = []
    for c in range(tb // CHUNK):
        sl = slice(c * CHUNK, (c + 1) * CHUNK)
        g_c = gates[sl]
        hi, lo = _split_hi_lo(g_c)
        cs = (jnp.dot(tri, hi, preferred_element_type=F32)
              + jnp.dot(tri, lo, preferred_element_type=F32))
        g_t = jnp.concatenate([g_c, zpad], axis=0).T
        cs_t = jnp.concatenate([cs, zpad], axis=0).T
        y_heads = []
        for pr in range(H_M // 2):
            h0, h1 = 2 * pr, 2 * pr + 1
            ls = slice(pr * LANES, (pr + 1) * LANES)
            q_st = stack2(q_all[sl, ls])
            k_sl = k_all[sl, ls]
            s_qk = _dot_nt(q_st, k_sl)
            b_col = jnp.concatenate([cs[:, H_M + h0:H_M + h0 + 1], cs[:, H_M + h1:H_M + h1 + 1]], axis=0)
            i_col = jnp.concatenate([g_c[:, h0:h0 + 1], g_c[:, h1:h1 + 1]], axis=0)
            b_row = rows2(cs_t[H_M + h0:H_M + h0 + 1, 0:CHUNK], cs_t[H_M + h1:H_M + h1 + 1, 0:CHUNK],
                          (CHUNK, CHUNK))
            i_row = rows2(g_t[h0:h0 + 1, 0:CHUNK], g_t[h1:h1 + 1, 0:CHUNK], (CHUNK, CHUNK))
            m_prev = m_st[pr]
            log_d = jnp.where(causal_st, b_col - b_row + i_row, -jnp.inf)
            m_inter = b_col + m_prev
            m_t = jnp.maximum(jnp.max(log_d, axis=1, keepdims=True), m_inter)
            s_d = s_qk * jnp.exp(log_d - m_t)
            sc = jnp.exp(m_inter - m_t)
            v0 = v_all[sl, h0 * DV_M:(h0 + 1) * DV_M]
            v1 = v_all[sl, h1 * DV_M:(h1 + 1) * DV_M]
            sv = jnp.concatenate([_dot(s_d[0:CHUNK], v0), _dot(s_d[CHUNK:n2], v1)], axis=0)
            num = sv + sc * _dot(q_st, c_st[pr])
            den = (jnp.sum(s_d, axis=1, keepdims=True)
                   + sc * jnp.sum(q_st * n_st[pr], axis=1, keepdims=True))
            hout = num / jnp.maximum(jnp.abs(den), jnp.exp(-m_t))
            hn = hout * lax.rsqrt(jnp.mean(hout * hout, axis=1, keepdims=True) + NORM_EPS)
            y_heads.append(hn[0:CHUNK])
            y_heads.append(hn[CHUNK:n2])
            b_last = rows2(cs[CHUNK - 1:CHUNK, H_M + h0:H_M + h0 + 1],
                           cs[CHUNK - 1:CHUNK, H_M + h1:H_M + h1 + 1], (CHUNK, 1))
            a_w = b_last - b_col + i_col
            a_max = rows2(jnp.max(a_w[0:CHUNK], axis=0, keepdims=True),
                          jnp.max(a_w[CHUNK:n2], axis=0, keepdims=True), (CHUNK, 1))
            m_new = jnp.maximum(b_last + m_prev, a_max)
            w_s = jnp.exp(a_w - m_new)
            dec = jnp.exp(b_last + m_prev - m_new)
            kw = k_sl * jnp.where(lane_lo, w_s[0:CHUNK], w_s[CHUNK:n2])
            dec_row = jnp.where(lane_lo[0:1], dec[0:1], dec[CHUNK:CHUNK + 1])
            n_st[pr] = dec_row * n_st[pr] + jnp.sum(kw, axis=0, keepdims=True)
            row_lo = lax.broadcasted_iota(jnp.int32, (n2, LANES), 0) < CHUNK
            c_st[pr] = dec * c_st[pr] + jnp.where(row_lo, _dot_tn(kw, v0), _dot_tn(kw, v1))
            m_st[pr] = m_new
        y_rows.append(jnp.concatenate(y_heads, axis=1))
    for pr in range(H_M // 2):
        c_ref[pr] = c_st[pr]
        n_ref[pr] = n_st[pr]
        m_ref[pr] = m_st[pr]
    y = jnp.concatenate(y_rows, axis=0)
    o_ref[...] = (y * ng_ref[...] * _sigmoid(o_gate)).astype(o_ref.dtype)


def _mlstm(p_ml, cw, cb, gb, ng, tb):
    bsz, t, _ = p_ml.shape
    const = lambda shape: pl.BlockSpec(shape, lambda b, i: (0,) * len(shape))
    return pl.pallas_call(
        functools.partial(_mlstm_kernel, tb=tb),
        grid=(bsz, t // tb),
        in_specs=[
            pl.BlockSpec((None, tb, ML_PAD), lambda b, i: (b, i, 0)),
            const((QK_CONV, 2 * C_MQK)),
            const((1, 2 * C_MQK)),
            const((1, LANES)),
            const((1, C_MV)),
        ],
        out_specs=pl.BlockSpec((None, tb, C_MV), lambda b, i: (b, i, 0)),
        out_shape=jax.ShapeDtypeStruct((bsz, t, C_MV), BF16),
        scratch_shapes=[
            pltpu.VMEM((SUBLANES, 2 * C_MQK), F32),
            pltpu.VMEM((H_M // 2, 2 * DK_M, DV_M), F32),
            pltpu.VMEM((H_M // 2, 1, 2 * DK_M), F32),
            pltpu.VMEM((H_M // 2, 2 * CHUNK, 1), F32),
        ],
        compiler_params=pltpu.CompilerParams(
            dimension_semantics=("parallel", "arbitrary"), vmem_limit_bytes=VMEM_LIMIT),
        name="mlstm",
    )(p_ml, cw, cb, gb, ng)


def _ffn_kernel(x_ref, yr_ref, ym_ref, wo_ref, g2_ref, wup_ref, cw_ref, cb_ref, wdn_ref, gf_ref,
                o_ref, x1_ref, h2_ref, act_ref, carry_ref, *, tm, tiles_per_seq):
    i = pl.program_id(0)

    @pl.when(i % tiles_per_seq == 0)
    def _():
        carry_ref[...] = jnp.zeros_like(carry_ref)

    x1 = (x_ref[...]
          + jnp.dot(yr_ref[...], wo_ref[0:C_R, :], preferred_element_type=F32)
          + jnp.dot(ym_ref[...], wo_ref[C_R:C_R + C_MV, :], preferred_element_type=F32))
    x1_ref[...] = x1
    h2_ref[...] = _rmsnorm_rows(x1, g2_ref[...]).astype(BF16)

    fc = 256
    for j in range(D_FF // fc):
        cs = slice(j * fc, (j + 1) * fc)
        h2 = h2_ref[...]
        a = jnp.dot(h2, wup_ref[:, j * fc:(j + 1) * fc], preferred_element_type=F32)
        b = jnp.dot(h2, wup_ref[:, D_FF + j * fc:D_FF + (j + 1) * fc], preferred_element_type=F32)
        ext = jnp.concatenate([carry_ref[:, cs], a], axis=0)
        carry_ref[:, cs] = a[tm - SUBLANES:tm, :]
        conv = cb_ref[:, cs] + a * cw_ref[FFN_CONV - 1:FFN_CONV, cs]
        for s in range(1, FFN_CONV):
            sh = pltpu.roll(ext, s, axis=0)[SUBLANES:SUBLANES + tm]
            conv = conv + sh * cw_ref[FFN_CONV - 1 - s:FFN_CONV - s, cs]
        act_ref[:, cs] = (conv * _sigmoid(conv) * b).astype(BF16)

    nc = 256
    for j in range(D_MODEL // nc):
        cs = slice(j * nc, (j + 1) * nc)
        x1_ref[:, cs] = x1_ref[:, cs] + jnp.dot(act_ref[...], wdn_ref[:, cs],
                                                preferred_element_type=F32)
    o_ref[...] = _rmsnorm_rows(x1_ref[...], gf_ref[...])


def _ffn(x2d, y_rw, y_ml, w_out, g2, w_up, cw, cb, w_dn, gf, tm, seq):
    n = x2d.shape[0]
    const = lambda shape: pl.BlockSpec(shape, lambda i: (0,) * len(shape),
                                       pipeline_mode=pl.Buffered(1))
    return pl.pallas_call(
        functools.partial(_ffn_kernel, tm=tm, tiles_per_seq=seq // tm),
        grid=(n // tm,),
        in_specs=[
            pl.BlockSpec((tm, D_MODEL), lambda i: (i, 0)),
            pl.BlockSpec((tm, C_R), lambda i: (i, 0)),
            pl.BlockSpec((tm, C_MV), lambda i: (i, 0)),
            const((C_R + C_MV, D_MODEL)),
            const((1, D_MODEL)),
            const((D_MODEL, 2 * D_FF)),
            const((FFN_CONV, D_FF)),
            const((1, D_FF)),
            const((D_FF, D_MODEL)),
            const((1, D_MODEL)),
        ],
        out_specs=pl.BlockSpec((tm, D_MODEL), lambda i: (i, 0)),
        out_shape=jax.ShapeDtypeStruct((n, D_MODEL), F32),
        scratch_shapes=[
            pltpu.VMEM((tm, D_MODEL), F32),
            pltpu.VMEM((tm, D_MODEL), BF16),
            pltpu.VMEM((tm, D_FF), BF16),
            pltpu.VMEM((SUBLANES, D_FF), F32),
        ],
        compiler_params=pltpu.CompilerParams(
            dimension_semantics=("arbitrary",), vmem_limit_bytes=VMEM_LIMIT),
        name="outproj_convffn",
    )(x2d, y_rw, y_ml, w_out, g2, w_up, cw, cb, w_dn, gf)


def _row(v):
    return v.reshape(1, -1)


def _layer(x, norm1_g, w_in, rw_mu, w0, w_up_decay, a0, w_up_a, w_up_g, k_k, k_a, r_k,
           lnx_w, lnx_b, qk_conv_w, qk_conv_b, i_bias, f_bias, mh_norm_g, w_out,
           norm2_g, w_ffn_up, ffn_conv_w, ffn_conv_b, w_ffn_down, out_g):
    bsz, seq, _ = x.shape
    n = bsz * seq
    tm = min(512, seq)
    tb = min(256, seq)
    x2d = x.reshape(n, D_MODEL)

    w_in_p = jnp.pad(w_in, ((0, 0), (0, ML_PAD - ML_COLS))).astype(BF16)
    p_rw, p_ml = _inproj(x2d, _row(norm1_g), w_in_p, tm)

    zeros_lora = jnp.zeros((DECAY_LORA, C_R), F32)
    wd = jnp.concatenate([w_up_decay, zeros_lora], axis=0).astype(BF16)
    wa = jnp.concatenate([zeros_lora, w_up_a], axis=0).astype(BF16)
    vecs = jnp.stack([w0, a0, k_k, k_a, r_k.reshape(C_R), lnx_w, lnx_b, jnp.zeros_like(w0)], axis=0)
    head_id = jnp.arange(C_R) // N_R
    ones_bd = (head_id[:, None] == head_id[None, :]).astype(BF16)
    y_rw = _rwkv(p_rw.reshape(bsz, seq, RW_COLS), _row(rw_mu), vecs, wd, wa,
                 w_up_g.astype(BF16), ones_bd, tb)

    gate_bias = jnp.pad(jnp.concatenate([i_bias, f_bias]), (0, LANES - 2 * H_M))
    y_ml = _mlstm(p_ml.reshape(bsz, seq, ML_PAD), qk_conv_w, _row(qk_conv_b), _row(gate_bias),
                  _row(mh_norm_g), tb)

    out = _ffn(x2d, y_rw.reshape(n, C_R), y_ml.reshape(n, C_MV), w_out.astype(BF16), _row(norm2_g),
               w_ffn_up.astype(BF16), ffn_conv_w, _row(ffn_conv_b), w_ffn_down.astype(BF16),
               _row(out_g), tm, seq)
    return out.reshape(bsz, seq, D_MODEL)


def kernel(x, norm1_g, w_in, rw_mu, w0, w_up_decay, a0, w_up_a, w_up_g, k_k, k_a, r_k, lnx_w, lnx_b,
           qk_conv_w, qk_conv_b, i_bias, f_bias, mh_norm_g, w_out, norm2_g, w_ffn_up, ffn_conv_w,
           ffn_conv_b, w_ffn_down, norm_f_g):
    assert norm1_g.shape[0] == 1
    return _layer(x, norm1_g[0], w_in[0], rw_mu[0], w0[0], w_up_decay[0], a0[0], w_up_a[0],
                  w_up_g[0], k_k[0], k_a[0], r_k[0], lnx_w[0], lnx_b[0], qk_conv_w[0],
                  qk_conv_b[0], i_bias[0], f_bias[0], mh_norm_g[0], w_out[0], norm2_g[0],
                  w_ffn_up[0], ffn_conv_w[0], ffn_conv_b[0], w_ffn_down[0], norm_f_g)
```

```python
import functools
import math

import jax
import jax.numpy as jnp
from jax import lax
from jax.experimental import pallas as pl
from jax.experimental.pallas import tpu as pltpu

F32 = jnp.float32
BF16 = jnp.bfloat16

D_MODEL = 1024
H_R, N_R = 8, 64
C_R = H_R * N_R
DECAY_LORA, AAA_LORA, GATE_LORA = 64, 64, 128
RW_COLS = 3 * C_R + DECAY_LORA + AAA_LORA + GATE_LORA
LNX_EPS = 64e-5
H_M, DK_M, DV_M = 4, 64, 128
C_MQK = H_M * DK_M
C_MV = H_M * DV_M
ML_COLS = 2 * C_MQK + 2 * C_MV + 2 * H_M
QK_CONV = 4
GATE_CAP = 15.0
D_FF = 2816
FFN_CONV = 3
NORM_EPS = 1e-6
CHUNK = 64

LANES = 128
SUBLANES = 8
ML_PAD = 1664
GATE_OFF = 2 * C_MQK + 2 * C_MV
VMEM_LIMIT = 56 * 1024 * 1024

EXP_M05 = math.exp(-0.5)


def _dot(a, b):
    return jnp.dot(a.astype(BF16), b.astype(BF16), preferred_element_type=F32)


def _dot_nt(a, b):
    return lax.dot_general(a.astype(BF16), b.astype(BF16), (((1,), (1,)), ((), ())),
                           preferred_element_type=F32)


def _dot_tn(a, b):
    return lax.dot_general(a.astype(BF16), b.astype(BF16), (((0,), (0,)), ((), ())),
                           preferred_element_type=F32)


def _split_hi_lo(x):
    hi = x.astype(BF16)
    lo = (x - hi.astype(F32)).astype(BF16)
    return hi, lo


def _sigmoid(x):
    return 1.0 / (1.0 + jnp.exp(-x))


def _rmsnorm_rows(x, g):
    ms = jnp.mean(x * x, axis=-1, keepdims=True)
    return x * lax.rsqrt(ms + NORM_EPS) * g


def _inproj_kernel(x_ref, g_ref, w_ref, orw_ref, oml_ref):
    h = _rmsnorm_rows(x_ref[...], g_ref[...]).astype(BF16)
    step = 256
    for c in range(0, RW_COLS, step):
        orw_ref[:, c:c + step] = jnp.dot(h, w_ref[:, c:c + step], preferred_element_type=F32)
    for c in range(0, ML_PAD, step):
        n = min(step, ML_PAD - c)
        oml_ref[:, c:c + n] = jnp.dot(h, w_ref[:, RW_COLS + c:RW_COLS + c + n],
                                      preferred_element_type=F32)


def _inproj(x2d, g1, w_in_p, tm):
    n = x2d.shape[0]
    return pl.pallas_call(
        _inproj_kernel,
        grid=(n // tm,),
        in_specs=[
            pl.BlockSpec((tm, D_MODEL), lambda i: (i, 0)),
            pl.BlockSpec((1, D_MODEL), lambda i: (0, 0)),
            pl.BlockSpec((D_MODEL, RW_COLS + ML_PAD), lambda i: (0, 0)),
        ],
        out_specs=[
            pl.BlockSpec((tm, RW_COLS), lambda i: (i, 0)),
            pl.BlockSpec((tm, ML_PAD), lambda i: (i, 0)),
        ],
        out_shape=[
            jax.ShapeDtypeStruct((n, RW_COLS), F32),
            jax.ShapeDtypeStruct((n, ML_PAD), F32),
        ],
        compiler_params=pltpu.CompilerParams(
            dimension_semantics=("parallel",), vmem_limit_bytes=VMEM_LIMIT),
        name="inproj",
    )(x2d, g1, w_in_p)


def _rwkv_kernel(p_ref, mu_ref, vec_ref, wd_ref, wa_ref, wg_ref, o_ref,
                 carry_ref, state_ref, *, tb):
    t_idx = pl.program_id(1)

    @pl.when(t_idx == 0)
    def _():
        carry_ref[...] = jnp.zeros_like(carry_ref)
        state_ref[...] = jnp.zeros_like(state_ref)

    p = p_ref[...]
    prev_last = carry_ref[SUBLANES - 1:SUBLANES, :]
    rolled = pltpu.roll(p, 1, axis=0)
    row0 = lax.broadcasted_iota(jnp.int32, (tb, 1), 0) == 0
    shifted = jnp.where(row0, prev_last, rolled)
    carry_ref[...] = p[tb - SUBLANES:tb, :]
    pm = p + (shifted - p) * mu_ref[...]

    w0 = vec_ref[0:1, :]
    a0 = vec_ref[1:2, :]
    k_k = vec_ref[2:3, :]
    k_a = vec_ref[3:4, :]
    r_k = vec_ref[4:5, :]
    lnx_w = vec_ref[5:6, :]
    lnx_b = vec_ref[6:7, :]

    r = pm[:, 0:C_R]
    k = pm[:, C_R:2 * C_R]
    v = pm[:, 2 * C_R:3 * C_R]
    z = pm[:, 3 * C_R:3 * C_R + LANES]
    g_lo = pm[:, 3 * C_R + LANES:RW_COLS]

    u = w0 + _dot(jnp.tanh(z), wd_ref[...])
    ld = -EXP_M05 * _sigmoid(u)
    a = _sigmoid(a0 + _dot(z, wa_ref[...]))
    g = _dot(_sigmoid(g_lo), wg_ref[...])
    kk = k * k_k
    kmod = k * (1.0 + (a - 1.0) * k_a)
    rkr = r * kmod * r_k

    n2 = 2 * CHUNK
    ri = lax.broadcasted_iota(jnp.int32, (n2, n2), 0)
    ci = lax.broadcasted_iota(jnp.int32, (n2, n2), 1)
    strict_lower = ri > ci
    lower = ri >= ci
    eye = (ri == ci).astype(F32)
    head_lanes = (ri < CHUNK) == (ci < N_R)
    lvl_masks = []
    for l in range(6):
        lvl_masks.append(((ri >> (l + 1)) == (ci >> (l + 1)))
                         & (((ri >> l) & 1) == 1) & (((ci >> l) & 1) == 0))
    lane_lo = lax.broadcasted_iota(jnp.int32, (CHUNK, LANES), 1) < N_R
    tri = (lax.broadcasted_iota(jnp.int32, (CHUNK, CHUNK), 0)
           >= lax.broadcasted_iota(jnp.int32, (CHUNK, CHUNK), 1)).astype(BF16)

    def stack2(x):
        return jnp.concatenate([jnp.where(lane_lo, x, 0.0), jnp.where(lane_lo, 0.0, x)], axis=0)

    def rep2(x):
        return jnp.concatenate([x, x], axis=0)

    n_chunks = tb // CHUNK
    n_pairs = H_R // 2
    zeros_sq = jnp.zeros((n2, n2), BF16)
    inv_n = 1.0 / N_R

    prob = []
    for c in range(n_chunks):
        sl = slice(c * CHUNK, (c + 1) * CHUNK)
        ld_c = ld[sl]
        hi, lo = _split_hi_lo(ld_c)
        cs = (jnp.dot(tri, hi, preferred_element_type=F32)
              + jnp.dot(tri, lo, preferred_element_type=F32))
        g_in = jnp.exp(cs)
        g_ex = jnp.exp(cs - ld_c)
        g_inv = jnp.exp(-cs)
        rt = r[sl] * g_in
        kt = kmod[sl] * g_inv
        ag = a[sl] * g_inv
        g_last = g_in[CHUNK - 1:CHUNK, :]
        for pr in range(n_pairs):
            ls = slice(pr * LANES, (pr + 1) * LANES)
            kk_st = stack2(kk[sl, ls])
            kkn_st = kk_st * jnp.minimum(lax.rsqrt(jnp.sum(kk_st * kk_st, axis=1, keepdims=True)), 1e12)
            p_a = (-kkn_st * rep2(g_ex[:, ls])).astype(BF16)
            p_b = (kkn_st * rep2(ag[:, ls])).astype(BF16)
            p_r = stack2(rt[:, ls]).astype(BF16)
            p_k = stack2(kt[:, ls]).astype(BF16)
            v_f = stack2(v[sl, ls])
            v_st = v_f.astype(BF16)
            bonus = jnp.sum(stack2(rkr[sl, ls]), axis=1, keepdims=True) * v_f
            p_bk = jnp.concatenate([p_b, p_k], axis=0)
            gram = lax.dot_general(jnp.concatenate([p_a, p_r], axis=0), p_bk,
                                   (((1,), (1,)), ((), ())), preferred_element_type=F32)
            g_ab = gram[0:n2, 0:n2]
            gl_col = jnp.broadcast_to(g_last[:, ls], (n2, n2)).T
            prob.append(dict(
                p_a=p_a, p_r=p_r, p_bk=p_bk, v_st=v_st, g_ab=g_ab, bonus=bonus,
                a_ak=jnp.where(strict_lower, gram[0:n2, n2:2 * n2], 0.0).astype(BF16),
                a_rb=jnp.where(lower, gram[n2:2 * n2, 0:n2], 0.0).astype(BF16),
                a_rk=jnp.where(lower, gram[n2:2 * n2, n2:2 * n2], 0.0).astype(BF16),
                x=(eye + jnp.where(lvl_masks[0], g_ab, 0.0)).astype(BF16),
                g_last=gl_col))
    for l in range(1, 6):
        for q in prob:
            a_l = jnp.where(lvl_masks[l], q["g_ab"], 0.0).astype(BF16)
            q["t"] = jnp.dot(q["x"], a_l, preferred_element_type=F32).astype(BF16)
        for q in prob:
            q["x"] = q["x"] + jnp.dot(q["t"], q["x"], preferred_element_type=F32).astype(BF16)
    for q in prob:
        q["wv"] = jnp.dot(q["a_ak"], q["v_st"], preferred_element_type=F32).astype(BF16)
    for q in prob:
        xc = jnp.dot(q["x"], jnp.concatenate([q["wv"], q["p_a"]], axis=1), preferred_element_type=F32)
        q["xwv"] = xc[:, 0:n2].astype(BF16)
        q["xpa"] = xc[:, n2:2 * n2].astype(BF16)
    for q in prob:
        uv = jnp.concatenate([q["xwv"], q["v_st"]], axis=0)
        q["yv"] = jnp.dot(jnp.concatenate([q["a_rb"], q["a_rk"]], axis=1), uv,
                          preferred_element_type=F32)
        r_eff = q["p_r"].astype(F32) + jnp.dot(q["a_rb"], q["xpa"], preferred_element_type=F32)
        hb = lax.dot_general(q["p_bk"],
                             jnp.concatenate([uv, jnp.concatenate([q["xpa"], zeros_sq], axis=0)], axis=1),
                             (((0,), (0,)), ((), ())), preferred_element_type=F32)
        q["h"] = hb[:, 0:n2]
        q["lhs"] = jnp.concatenate([r_eff.astype(BF16), hb[:, n2:2 * n2].astype(BF16)], axis=0)

    states = [state_ref[i] for i in range(n_pairs)]
    for c in range(n_chunks):
        for pr in range(n_pairs):
            q = prob[c * n_pairs + pr]
            m0 = states[pr]
            o2 = jnp.dot(q["lhs"], m0.astype(BF16), preferred_element_type=F32)
            states[pr] = q["g_last"] * (m0 + o2[n2:2 * n2] + q["h"])
            q["y"] = o2[0:n2] + q["yv"]
    for q in prob:
        q["mean"] = jnp.sum(q["y"], axis=1, keepdims=True) * inv_n
    for q in prob:
        q["d"] = jnp.where(head_lanes, q["y"] - q["mean"], 0.0)
        q["var"] = jnp.sum(q["d"] * q["d"], axis=1, keepdims=True) * inv_n
    for c in range(n_chunks):
        sl = slice(c * CHUNK, (c + 1) * CHUNK)
        for pr in range(n_pairs):
            ls = slice(pr * LANES, (pr + 1) * LANES)
            q = prob[c * n_pairs + pr]
            comb = q["d"] * lax.rsqrt(q["var"] + LNX_EPS) * lnx_w[:, ls] + q["bonus"]
            y = comb[0:CHUNK] + comb[CHUNK:n2] + lnx_b[:, ls]
            o_ref[sl, ls] = (y * g[sl, ls]).astype(o_ref.dtype)
    for pr in range(n_pairs):
        state_ref[pr] = states[pr]


def _rwkv(p_rw, mu, vecs, wd, wa, wg, tb):
    bsz, t, _ = p_rw.shape
    const = lambda shape: pl.BlockSpec(shape, lambda b, i: (0,) * len(shape))
    return pl.pallas_call(
        functools.partial(_rwkv_kernel, tb=tb),
        grid=(bsz, t // tb),
        in_specs=[
            pl.BlockSpec((None, tb, RW_COLS), lambda b, i: (b, i, 0)),
            const((1, RW_COLS)),
            const((SUBLANES, C_R)),
            const((LANES, C_R)),
            const((LANES, C_R)),
            const((GATE_LORA, C_R)),
        ],
        out_specs=pl.BlockSpec((None, tb, C_R), lambda b, i: (b, i, 0)),
        out_shape=jax.ShapeDtypeStruct((bsz, t, C_R), BF16),
        scratch_shapes=[
            pltpu.VMEM((SUBLANES, RW_COLS), F32),
            pltpu.VMEM((H_R // 2, 2 * N_R, 2 * N_R), F32),
        ],
        compiler_params=pltpu.CompilerParams(
            dimension_semantics=("parallel", "arbitrary"), vmem_limit_bytes=VMEM_LIMIT),
        name="rwkv7",
    )(p_rw, mu, vecs, wd, wa, wg)


def _mlstm_kernel(p_ref, cw_ref, cb_ref, gb_ref, ng_ref, o_ref,
                  carry_ref, cn_ref, m_ref, *, tb):
    t_idx = pl.program_id(1)

    @pl.when(t_idx == 0)
    def _():
        carry_ref[...] = jnp.zeros_like(carry_ref)
        cn_ref[...] = jnp.zeros_like(cn_ref)
        m_ref[...] = jnp.zeros_like(m_ref)

    nqk = 2 * C_MQK
    qk_in = p_ref[:, 0:nqk]
    ext = jnp.concatenate([carry_ref[...], qk_in], axis=0)
    carry_ref[...] = qk_in[tb - SUBLANES:tb, :]
    acc = cb_ref[...] + qk_in * cw_ref[QK_CONV - 1:QK_CONV, :]
    for s in range(1, QK_CONV):
        sh = pltpu.roll(ext, s, axis=0)[SUBLANES:SUBLANES + tb]
        acc = acc + sh * cw_ref[QK_CONV - 1 - s:QK_CONV - s, :]
    qk = acc * _sigmoid(acc)
    q_all = qk[:, 0:C_MQK] * (DK_M ** -0.5)
    k_all = qk[:, C_MQK:nqk]
    v_all = p_ref[:, nqk:nqk + C_MV]
    o_gate = p_ref[:, nqk + C_MV:nqk + 2 * C_MV]

    graw = p_ref[:, GATE_OFF:GATE_OFF + LANES] + gb_ref[...]
    capped = GATE_CAP * jnp.tanh(graw / GATE_CAP)
    logsig = jnp.minimum(capped, 0.0) - jnp.log(1.0 + jnp.exp(-jnp.abs(capped)))
    lane_g = lax.broadcasted_iota(jnp.int32, (tb, LANES), 1)
    gates = jnp.where(lane_g < H_M, capped, logsig)

    ig_sh = pltpu.roll(gates, H_M, axis=1)

    n2 = 2 * CHUNK
    tri = (lax.broadcasted_iota(jnp.int32, (CHUNK, CHUNK), 0)
           >= lax.broadcasted_iota(jnp.int32, (CHUNK, CHUNK), 1)).astype(BF16)
    lane_lo = lax.broadcasted_iota(jnp.int32, (CHUNK, LANES), 1) < DK_M
    causal_st = ((lax.broadcasted_iota(jnp.int32, (n2, CHUNK), 0) & (CHUNK - 1))
                 >= lax.broadcasted_iota(jnp.int32, (n2, CHUNK), 1))
    zpad = jnp.zeros((CHUNK, LANES), F32)
    ones_v = jnp.ones((CHUNK, DV_M), BF16)
    ones_sq = jnp.ones((DV_M, DV_M), BF16)
    row_sq = lax.broadcasted_iota(jnp.int32, (n2, LANES), 0)
    lane_sq = lax.broadcasted_iota(jnp.int32, (n2, LANES), 1)
    row_lo = row_sq < CHUNK
    n_chunks = tb // CHUNK
    n_pairs = H_M // 2
    gate_lane = [lane_sq == jnp.where(row_lo, H_M + 2 * pr, H_M + 2 * pr + 1) for pr in range(n_pairs)]
    spread = [(row_sq == jnp.where(lane_sq < DK_M, H_M + 2 * pr, H_M + 2 * pr + 1)).astype(BF16)
              for pr in range(n_pairs)]

    def stack2(x):
        return jnp.concatenate([jnp.where(lane_lo, x, 0.0), jnp.where(lane_lo, 0.0, x)], axis=0)

    def rep2(x):
        return jnp.concatenate([x, x], axis=0)

    def rows2(x0, x1, shape):
        return jnp.concatenate([jnp.broadcast_to(x0, shape), jnp.broadcast_to(x1, shape)], axis=0)

    prob = []
    tn = (((0,), (0,)), ((), ()))
    for c in range(n_chunks):
        sl = slice(c * CHUNK, (c + 1) * CHUNK)
        hi, lo = _split_hi_lo(gates[sl])
        cs = (jnp.dot(tri, hi, preferred_element_type=F32)
              + jnp.dot(tri, lo, preferred_element_type=F32))
        c_all = ig_sh[sl] - cs
        w_all = jnp.exp(c_all - jnp.max(c_all, axis=0, keepdims=True)).astype(BF16)
        c_t = jnp.concatenate([c_all, zpad], axis=0).T
        cs2 = rep2(cs)
        for pr in range(n_pairs):
            prob.append(dict(c=c, pr=pr, sl=sl, w_all=w_all, c_t=c_t, cs2=cs2))
    for q in prob:
        ls = slice(q["pr"] * LANES, (q["pr"] + 1) * LANES)
        q["q_st"] = stack2(q_all[q["sl"], ls]).astype(BF16)
        q["k_sl"] = k_all[q["sl"], ls]
        q["s_qk"] = lax.dot_general(q["q_st"], q["k_sl"].astype(BF16), (((1,), (1,)), ((), ())),
                                    preferred_element_type=F32)
    for q in prob:
        h0, h1 = 2 * q["pr"], 2 * q["pr"] + 1
        q["b_col"] = jnp.sum(jnp.where(gate_lane[q["pr"]], q["cs2"], 0.0), axis=1, keepdims=True)
        c_row = rows2(q["c_t"][H_M + h0:H_M + h0 + 1, 0:CHUNK], q["c_t"][H_M + h1:H_M + h1 + 1, 0:CHUNK],
                      (CHUNK, CHUNK))
        q["masked"] = jnp.where(causal_st, c_row, -jnp.inf)
        q["cm"] = jnp.max(q["masked"], axis=1, keepdims=True)
    for q in prob:
        h0, h1 = 2 * q["pr"], 2 * q["pr"] + 1
        q["s_d"] = (q["s_qk"] * jnp.exp(q["masked"] - q["cm"])).astype(BF16)
        q["v0"] = jnp.concatenate([v_all[q["sl"], h0 * DV_M:(h0 + 1) * DV_M].astype(BF16), ones_v], axis=1)
        q["v1"] = jnp.concatenate([v_all[q["sl"], h1 * DV_M:(h1 + 1) * DV_M].astype(BF16), ones_v], axis=1)
        q["kw"] = (q["k_sl"] * jnp.dot(q["w_all"], spread[q["pr"]], preferred_element_type=F32)).astype(BF16)
    for q in prob:
        q["sv"] = jnp.concatenate([jnp.dot(q["s_d"][0:CHUNK], q["v0"], preferred_element_type=F32),
                                   jnp.dot(q["s_d"][CHUNK:n2], q["v1"], preferred_element_type=F32)], axis=0)
        q["kv"] = jnp.where(jnp.concatenate([row_lo, row_lo], axis=1),
                            lax.dot_general(q["kw"], q["v0"], tn, preferred_element_type=F32),
                            lax.dot_general(q["kw"], q["v1"], tn, preferred_element_type=F32))
        q["b_last"] = rows2(q["b_col"][CHUNK - 1:CHUNK], q["b_col"][n2 - 1:n2], (CHUNK, 1))
        q["a_max"] = q["b_last"] + rows2(q["cm"][CHUNK - 1:CHUNK], q["cm"][n2 - 1:n2], (CHUNK, 1))

    cn_st = [cn_ref[i] for i in range(n_pairs)]
    m_st = [m_ref[i] for i in range(n_pairs)]
    for q in prob:
        pr = q["pr"]
        q["m_prev"] = m_st[pr]
        q["cn_prev"] = cn_st[pr].astype(BF16)
        m_new = jnp.maximum(q["b_last"] + q["m_prev"], q["a_max"])
        dec = jnp.exp(q["b_last"] + q["m_prev"] - m_new)
        beta = jnp.exp(q["a_max"] - m_new)
        cn_st[pr] = dec * cn_st[pr] + beta * q["kv"]
        m_st[pr] = m_new
    for q in prob:
        q["mx"] = jnp.maximum(q["cm"], q["m_prev"])
        q["qc"] = jnp.dot(q["q_st"], q["cn_prev"], preferred_element_type=F32)
    for q in prob:
        mix = jnp.exp(q["cm"] - q["mx"]) * q["sv"] + jnp.exp(q["m_prev"] - q["mx"]) * q["qc"]
        num = mix[:, 0:DV_M]
        den = mix[:, DV_M:2 * DV_M]
        q["hout"] = num / jnp.maximum(jnp.abs(den), jnp.exp(-(q["b_col"] + q["mx"])))
    for q in prob:
        q["ms"] = jnp.dot((q["hout"] * q["hout"]).astype(BF16), ones_sq,
                          preferred_element_type=F32) * (1.0 / DV_M)
    y_rows = []
    for c in range(n_chunks):
        y_heads = []
        for pr in range(n_pairs):
            q = prob[c * n_pairs + pr]
            hn = q["hout"] * lax.rsqrt(q["ms"] + NORM_EPS)
            y_heads.append(hn[0:CHUNK])
            y_heads.append(hn[CHUNK:n2])
        y_rows.append(jnp.concatenate(y_heads, axis=1))
    for pr in range(n_pairs):
        cn_ref[pr] = cn_st[pr]
        m_ref[pr] = m_st[pr]
    y = jnp.concatenate(y_rows, axis=0)
    o_ref[...] = (y * ng_ref[...] * _sigmoid(o_gate)).astype(o_ref.dtype)


def _mlstm(p_ml, cw, cb, gb, ng, tb):
    bsz, t, _ = p_ml.shape
    const = lambda shape: pl.BlockSpec(shape, lambda b, i: (0,) * len(shape))
    return pl.pallas_call(
        functools.partial(_mlstm_kernel, tb=tb),
        grid=(bsz, t // tb),
        in_specs=[
            pl.BlockSpec((None, tb, ML_PAD), lambda b, i: (b, i, 0)),
            const((QK_CONV, 2 * C_MQK)),
            const((1, 2 * C_MQK)),
            const((1, LANES)),
            const((1, C_MV)),
        ],
        out_specs=pl.BlockSpec((None, tb, C_MV), lambda b, i: (b, i, 0)),
        out_shape=jax.ShapeDtypeStruct((bsz, t, C_MV), BF16),
        scratch_shapes=[
            pltpu.VMEM((SUBLANES, 2 * C_MQK), F32),
            pltpu.VMEM((H_M // 2, 2 * DK_M, 2 * DV_M), F32),
            pltpu.VMEM((H_M // 2, 2 * CHUNK, 1), F32),
        ],
        compiler_params=pltpu.CompilerParams(
            dimension_semantics=("parallel", "arbitrary"), vmem_limit_bytes=VMEM_LIMIT),
        name="mlstm",
    )(p_ml, cw, cb, gb, ng)


def _ffn_kernel(x_ref, yr_ref, ym_ref, wo_ref, g2_ref, wup_ref, cw_ref, cb_ref, wdn_ref, gf_ref,
                o_ref, x1_ref, h2_ref, act_ref, carry_ref, *, tm, tiles_per_seq):
    i = pl.program_id(0)

    @pl.when(i % tiles_per_seq == 0)
    def _():
        carry_ref[...] = jnp.zeros_like(carry_ref)

    x1 = (x_ref[...]
          + jnp.dot(yr_ref[...], wo_ref[0:C_R, :], preferred_element_type=F32)
          + jnp.dot(ym_ref[...], wo_ref[C_R:C_R + C_MV, :], preferred_element_type=F32))
    x1_ref[...] = x1
    h2_ref[...] = _rmsnorm_rows(x1, g2_ref[...]).astype(BF16)

    fc = 256
    for j in range(D_FF // fc):
        cs = slice(j * fc, (j + 1) * fc)
        h2 = h2_ref[...]
        a = jnp.dot(h2, wup_ref[:, j * fc:(j + 1) * fc], preferred_element_type=F32)
        b = jnp.dot(h2, wup_ref[:, D_FF + j * fc:D_FF + (j + 1) * fc], preferred_element_type=F32)
        ext = jnp.concatenate([carry_ref[:, cs], a], axis=0)
        carry_ref[:, cs] = a[tm - SUBLANES:tm, :]
        conv = cb_ref[:, cs] + a * cw_ref[FFN_CONV - 1:FFN_CONV, cs]
        for s in range(1, FFN_CONV):
            sh = pltpu.roll(ext, s, axis=0)[SUBLANES:SUBLANES + tm]
            conv = conv + sh * cw_ref[FFN_CONV - 1 - s:FFN_CONV - s, cs]
        act_ref[:, cs] = (conv * _sigmoid(conv) * b).astype(BF16)

    nc = 256
    for j in range(D_MODEL // nc):
        cs = slice(j * nc, (j + 1) * nc)
        x1_ref[:, cs] = x1_ref[:, cs] + jnp.dot(act_ref[...], wdn_ref[:, cs],
                                                preferred_element_type=F32)
    o_ref[...] = _rmsnorm_rows(x1_ref[...], gf_ref[...])


def _ffn(x2d, y_rw, y_ml, w_out, g2, w_up, cw, cb, w_dn, gf, tm, seq):
    n = x2d.shape[0]
    const = lambda shape: pl.BlockSpec(shape, lambda i: (0,) * len(shape),
                                       pipeline_mode=pl.Buffered(1))
    return pl.pallas_call(
        functools.partial(_ffn_kernel, tm=tm, tiles_per_seq=seq // tm),
        grid=(n // tm,),
        in_specs=[
            pl.BlockSpec((tm, D_MODEL), lambda i: (i, 0)),
            pl.BlockSpec((tm, C_R), lambda i: (i, 0)),
            pl.BlockSpec((tm, C_MV), lambda i: (i, 0)),
            const((C_R + C_MV, D_MODEL)),
            const((1, D_MODEL)),
            const((D_MODEL, 2 * D_FF)),
            const((FFN_CONV, D_FF)),
            const((1, D_FF)),
            const((D_FF, D_MODEL)),
            const((1, D_MODEL)),
        ],
        out_specs=pl.BlockSpec((tm, D_MODEL), lambda i: (i, 0)),
        out_shape=jax.ShapeDtypeStruct((n, D_MODEL), F32),
        scratch_shapes=[
            pltpu.VMEM((tm, D_MODEL), F32),
            pltpu.VMEM((tm, D_MODEL), BF16),
            pltpu.VMEM((tm, D_FF), BF16),
            pltpu.VMEM((SUBLANES, D_FF), F32),
        ],
        compiler_params=pltpu.CompilerParams(
            dimension_semantics=("arbitrary",), vmem_limit_bytes=VMEM_LIMIT),
        name="outproj_convffn",
    )(x2d, y_rw, y_ml, w_out, g2, w_up, cw, cb, w_dn, gf)


def _row(v):
    return v.reshape(1, -1)


def _layer(x, norm1_g, w_in, rw_mu, w0, w_up_decay, a0, w_up_a, w_up_g, k_k, k_a, r_k,
           lnx_w, lnx_b, qk_conv_w, qk_conv_b, i_bias, f_bias, mh_norm_g, w_out,
           norm2_g, w_ffn_up, ffn_conv_w, ffn_conv_b, w_ffn_down, out_g):
    bsz, seq, _ = x.shape
    n = bsz * seq
    tm = min(512, seq)
    tb = min(256, seq)
    x2d = x.reshape(n, D_MODEL)

    w_in_p = jnp.pad(w_in, ((0, 0), (0, ML_PAD - ML_COLS))).astype(BF16)
    p_rw, p_ml = _inproj(x2d, _row(norm1_g), w_in_p, tm)

    zeros_lora = jnp.zeros((DECAY_LORA, C_R), F32)
    wd = jnp.concatenate([w_up_decay, zeros_lora], axis=0).astype(BF16)
    wa = jnp.concatenate([zeros_lora, w_up_a], axis=0).astype(BF16)
    vecs = jnp.stack([w0, a0, k_k, k_a, r_k.reshape(C_R), lnx_w, lnx_b, jnp.zeros_like(w0)], axis=0)
    y_rw = _rwkv(p_rw.reshape(bsz, seq, RW_COLS), _row(rw_mu), vecs, wd, wa,
                 w_up_g.astype(BF16), tb)

    gate_bias = jnp.pad(jnp.concatenate([i_bias, f_bias]), (0, LANES - 2 * H_M))
    y_ml = _mlstm(p_ml.reshape(bsz, seq, ML_PAD), qk_conv_w, _row(qk_conv_b), _row(gate_bias),
                  _row(mh_norm_g), tb)

    out = _ffn(x2d, y_rw.reshape(n, C_R), y_ml.reshape(n, C_MV), w_out.astype(BF16), _row(norm2_g),
               w_ffn_up.astype(BF16), ffn_conv_w, _row(ffn_conv_b), w_ffn_down.astype(BF16),
               _row(out_g), tm, seq)
    return out.reshape(bsz, seq, D_MODEL)


def kernel(x, norm1_g, w_in, rw_mu, w0, w_up_decay, a0, w_up_a, w_up_g, k_k, k_a, r_k, lnx_w, lnx_b,
           qk_conv_w, qk_conv_b, i_bias, f_bias, mh_norm_g, w_out, norm2_g, w_ffn_up, ffn_conv_w,
           ffn_conv_b, w_ffn_down, norm_f_g):
    assert norm1_g.shape[0] == 1
    return _layer(x, norm1_g[0], w_in[0], rw_mu[0], w0[0], w_up_decay[0], a0[0], w_up_a[0],
                  w_up_g[0], k_k[0], k_a[0], r_k[0], lnx_w[0], lnx_b[0], qk_conv_w[0],
                  qk_conv_b[0], i_bias[0], f_bias[0], mh_norm_g[0], w_out[0], norm2_g[0],
                  w_ffn_up[0], ffn_conv_w[0], ffn_conv_b[0], w_ffn_down[0], norm_f_g)
```

```python
import functools
import math

import jax
import jax.numpy as jnp
from jax import lax
from jax.experimental import pallas as pl
from jax.experimental.pallas import tpu as pltpu

F32 = jnp.float32
BF16 = jnp.bfloat16

D_MODEL = 1024
H_R, N_R = 8, 64
C_R = H_R * N_R
DECAY_LORA, AAA_LORA, GATE_LORA = 64, 64, 128
RW_COLS = 3 * C_R + DECAY_LORA + AAA_LORA + GATE_LORA
LNX_EPS = 64e-5
H_M, DK_M, DV_M = 4, 64, 128
C_MQK = H_M * DK_M
C_MV = H_M * DV_M
ML_COLS = 2 * C_MQK + 2 * C_MV + 2 * H_M
QK_CONV = 4
GATE_CAP = 15.0
D_FF = 2816
FFN_CONV = 3
NORM_EPS = 1e-6
CHUNK = 64

LANES = 128
SUBLANES = 8
BF16_ROWS = 16
ML_PAD = 1664
GATE_OFF = 2 * C_MQK + 2 * C_MV
VMEM_LIMIT = 56 * 1024 * 1024

EXP_M05 = math.exp(-0.5)


def _dot(a, b):
    return jnp.dot(a.astype(BF16), b.astype(BF16), preferred_element_type=F32)


def _split_hi_lo(x):
    hi = x.astype(BF16)
    lo = (x - hi.astype(F32)).astype(BF16)
    return hi, lo


def _sigmoid(x):
    return 1.0 / (1.0 + jnp.exp(-x))


def _rmsnorm_rows(x, g):
    ms = jnp.mean(x * x, axis=-1, keepdims=True)
    return x * lax.rsqrt(ms + NORM_EPS) * g


def _inproj_kernel(x_ref, g_ref, w_ref, mu_ref, cw_ref, cb_ref, orw_ref, oml_ref,
                   rcarry_ref, mcarry_ref, *, tm, tiles_per_seq):
    i = pl.program_id(0)

    @pl.when(i % tiles_per_seq == 0)
    def _():
        rcarry_ref[...] = jnp.zeros_like(rcarry_ref)
        mcarry_ref[...] = jnp.zeros_like(mcarry_ref)

    h = _rmsnorm_rows(x_ref[...], g_ref[...]).astype(BF16)
    row0 = lax.broadcasted_iota(jnp.int32, (tm, 1), 0) == 0
    step = 256
    for c in range(0, RW_COLS, step):
        cs = slice(c, c + step)
        p = jnp.dot(h, w_ref[:, cs], preferred_element_type=F32)
        shifted = jnp.where(row0, rcarry_ref[SUBLANES - 1:SUBLANES, cs], pltpu.roll(p, 1, axis=0))
        rcarry_ref[:, cs] = p[tm - SUBLANES:tm, :]
        orw_ref[:, cs] = p + (shifted - p) * mu_ref[:, cs]
    nqk = 2 * C_MQK
    for c in range(0, nqk, step):
        cs = slice(c, c + step)
        p = jnp.dot(h, w_ref[:, RW_COLS + c:RW_COLS + c + step], preferred_element_type=F32)
        ext = jnp.concatenate([mcarry_ref[:, cs], p], axis=0)
        mcarry_ref[:, cs] = p[tm - SUBLANES:tm, :]
        acc = cb_ref[:, cs] + p * cw_ref[QK_CONV - 1:QK_CONV, cs]
        for s in range(1, QK_CONV):
            sh = pltpu.roll(ext, s, axis=0)[SUBLANES:SUBLANES + tm]
            acc = acc + sh * cw_ref[QK_CONV - 1 - s:QK_CONV - s, cs]
        qk = acc * _sigmoid(acc)
        if c < C_MQK:
            qk = qk * (DK_M ** -0.5)
        oml_ref[:, cs] = qk
    for c in range(nqk, ML_PAD, step):
        n = min(step, ML_PAD - c)
        oml_ref[:, c:c + n] = jnp.dot(h, w_ref[:, RW_COLS + c:RW_COLS + c + n],
                                      preferred_element_type=F32)


def _inproj(x2d, g1, w_in_p, mu, cw, cb, tm, seq):
    n = x2d.shape[0]
    const = lambda shape: pl.BlockSpec(shape, lambda i: (0,) * len(shape))
    return pl.pallas_call(
        functools.partial(_inproj_kernel, tm=tm, tiles_per_seq=seq // tm),
        grid=(n // tm,),
        in_specs=[
            pl.BlockSpec((tm, D_MODEL), lambda i: (i, 0)),
            const((1, D_MODEL)),
            const((D_MODEL, RW_COLS + ML_PAD)),
            const((1, RW_COLS)),
            const((QK_CONV, 2 * C_MQK)),
            const((1, 2 * C_MQK)),
        ],
        out_specs=[
            pl.BlockSpec((tm, RW_COLS), lambda i: (i, 0)),
            pl.BlockSpec((tm, ML_PAD), lambda i: (i, 0)),
        ],
        out_shape=[
            jax.ShapeDtypeStruct((n, RW_COLS), F32),
            jax.ShapeDtypeStruct((n, ML_PAD), F32),
        ],
        scratch_shapes=[
            pltpu.VMEM((SUBLANES, RW_COLS), F32),
            pltpu.VMEM((SUBLANES, 2 * C_MQK), F32),
        ],
        compiler_params=pltpu.CompilerParams(
            dimension_semantics=("arbitrary",), vmem_limit_bytes=VMEM_LIMIT),
        name="inproj",
    )(x2d, g1, w_in_p, mu, cw, cb)


def _rwkv_stages(p_ref, vec_ref, wd_ref, wa_ref, wg_ref, o_ref, state_ref, tb):
    t_idx = pl.program_id(1)

    @pl.when(t_idx == 0)
    def _():
        state_ref[...] = jnp.zeros_like(state_ref)

    w0 = vec_ref[0:1, :]
    a0 = vec_ref[1:2, :]
    k_k = vec_ref[2:3, :]
    k_a = vec_ref[3:4, :]
    r_k = vec_ref[4:5, :]
    lnx_w = vec_ref[5:6, :]
    lnx_b = vec_ref[6:7, :]

    r = p_ref[:, 0:C_R]
    k = p_ref[:, C_R:2 * C_R]
    v = p_ref[:, 2 * C_R:3 * C_R]
    z = p_ref[:, 3 * C_R:3 * C_R + LANES]
    g_lo = p_ref[:, 3 * C_R + LANES:RW_COLS]

    u = w0 + _dot(jnp.tanh(z), wd_ref[...])
    ld = -EXP_M05 * _sigmoid(u)
    a = _sigmoid(a0 + _dot(z, wa_ref[...]))
    g = _dot(_sigmoid(g_lo), wg_ref[...])
    kk = k * k_k
    kmod = k * (1.0 + (a - 1.0) * k_a)
    rkr = r * kmod * r_k
    yield

    n2 = 2 * CHUNK
    ri = lax.broadcasted_iota(jnp.int32, (n2, n2), 0)
    ci = lax.broadcasted_iota(jnp.int32, (n2, n2), 1)
    strict_lower = ri > ci
    lower = ri >= ci
    eye = (ri == ci).astype(F32).astype(BF16)
    head_lanes = (ri < CHUNK) == (ci < N_R)
    lvl_sel = []
    for l in range(6):
        m = (((ri >> (l + 1)) == (ci >> (l + 1))) & (((ri >> l) & 1) == 1) & (((ci >> l) & 1) == 0))
        lvl_sel.append(m.astype(F32).astype(BF16))
    lane_lo = lax.broadcasted_iota(jnp.int32, (CHUNK, LANES), 1) < N_R
    tri = (lax.broadcasted_iota(jnp.int32, (CHUNK, CHUNK), 0)
           >= lax.broadcasted_iota(jnp.int32, (CHUNK, CHUNK), 1)).astype(BF16)

    def stack2(x):
        return jnp.concatenate([jnp.where(lane_lo, x, 0.0), jnp.where(lane_lo, 0.0, x)], axis=0)

    def rep2(x):
        return jnp.concatenate([x, x], axis=0)

    n_chunks = tb // CHUNK
    n_pairs = H_R // 2
    zeros_sq = jnp.zeros((n2, n2), BF16)
    inv_n = 1.0 / N_R

    prob = []
    for c in range(n_chunks):
        sl = slice(c * CHUNK, (c + 1) * CHUNK)
        ld_c = ld[sl]
        hi, lo = _split_hi_lo(ld_c)
        cs = (jnp.dot(tri, hi, preferred_element_type=F32)
              + jnp.dot(tri, lo, preferred_element_type=F32))
        g_in = jnp.exp(cs)
        g_ex = jnp.exp(cs - ld_c)
        g_inv = jnp.exp(-cs)
        rt = r[sl] * g_in
        kt = kmod[sl] * g_inv
        ag = a[sl] * g_inv
        g_last = g_in[CHUNK - 1:CHUNK, :]
        for pr in range(n_pairs):
            ls = slice(pr * LANES, (pr + 1) * LANES)
            kk_st = stack2(kk[sl, ls])
            kkn_st = kk_st * jnp.minimum(lax.rsqrt(jnp.sum(kk_st * kk_st, axis=1, keepdims=True)), 1e12)
            p_a = (-kkn_st * rep2(g_ex[:, ls])).astype(BF16)
            p_b = (kkn_st * rep2(ag[:, ls])).astype(BF16)
            p_r = stack2(rt[:, ls]).astype(BF16)
            p_k = stack2(kt[:, ls]).astype(BF16)
            v_f = stack2(v[sl, ls])
            v_st = v_f.astype(BF16)
            bonus = jnp.sum(stack2(rkr[sl, ls]), axis=1, keepdims=True) * v_f
            p_bk = jnp.concatenate([p_b, p_k], axis=0)
            gram = lax.dot_general(jnp.concatenate([p_a, p_r], axis=0), p_bk,
                                   (((1,), (1,)), ((), ())), preferred_element_type=F32)
            g_ab = gram[0:n2, 0:n2].astype(BF16)
            gl_col = jnp.broadcast_to(g_last[:, ls], (n2, n2)).T
            prob.append(dict(
                p_a=p_a, p_r=p_r, p_bk=p_bk, v_st=v_st, g_ab=g_ab, bonus=bonus,
                a_ak=jnp.where(strict_lower, gram[0:n2, n2:2 * n2], 0.0).astype(BF16),
                a_rb=jnp.where(lower, gram[n2:2 * n2, 0:n2], 0.0).astype(BF16),
                a_rk=jnp.where(lower, gram[n2:2 * n2, n2:2 * n2], 0.0).astype(BF16),
                x=eye + g_ab * lvl_sel[0],
                g_last=gl_col))
        yield
    def odd_blocks(x, s):
        return jnp.concatenate([x[i * s:(i + 1) * s] for i in range(1, n2 // s, 2)], axis=0)

    def add_odd_blocks(x, u, s):
        parts = []
        for i in range(n2 // s):
            blk = x[i * s:(i + 1) * s]
            parts.append(blk + u[(i // 2) * s:(i // 2 + 1) * s] if i % 2 else blk)
        return jnp.concatenate(parts, axis=0)

    for l in range(1, 6):
        s = 1 << l
        sub = s >= BF16_ROWS
        for q in prob:
            lhs = odd_blocks(q["x"], s) if sub else q["x"]
            q["t"] = jnp.dot(lhs, q["g_ab"] * lvl_sel[l], preferred_element_type=F32).astype(BF16)
        yield
        for q in prob:
            u = jnp.dot(q["t"], q["x"], preferred_element_type=F32).astype(BF16)
            q["x"] = add_odd_blocks(q["x"], u, s) if sub else q["x"] + u
        yield
    for q in prob:
        s1 = jnp.dot(jnp.concatenate([q["a_ak"], q["a_rk"]], axis=0), q["v_st"],
                     preferred_element_type=F32)
        q["wv"] = s1[0:n2].astype(BF16)
        q["arkv"] = s1[n2:2 * n2]
    yield
    for q in prob:
        q["xc"] = jnp.dot(q["x"], jnp.concatenate([q["wv"], q["p_a"]], axis=1),
                          preferred_element_type=F32).astype(BF16)
    yield
    for q in prob:
        s2 = jnp.dot(q["a_rb"], q["xc"], preferred_element_type=F32)
        q["yv"] = s2[:, 0:n2] + q["arkv"]
        r_eff = q["p_r"].astype(F32) + s2[:, n2:2 * n2]
        rhs = jnp.concatenate([q["xc"], jnp.concatenate([q["v_st"], zeros_sq], axis=1)], axis=0)
        hb = lax.dot_general(q["p_bk"], rhs, (((0,), (0,)), ((), ())),
                             preferred_element_type=F32)
        q["h"] = hb[:, 0:n2]
        q["lhs"] = jnp.concatenate([r_eff.astype(BF16), hb[:, n2:2 * n2].astype(BF16)], axis=0)
    yield

    states = [state_ref[i] for i in range(n_pairs)]
    for c in range(n_chunks):
        for pr in range(n_pairs):
            q = prob[c * n_pairs + pr]
            m0 = states[pr]
            o2 = jnp.dot(q["lhs"], m0.astype(BF16), preferred_element_type=F32)
            states[pr] = q["g_last"] * (m0 + o2[n2:2 * n2] + q["h"])
            q["y"] = o2[0:n2] + q["yv"]
        yield
    for pr in range(n_pairs):
        state_ref[pr] = states[pr]
    for q in prob:
        q["mean"] = jnp.sum(q["y"], axis=1, keepdims=True) * inv_n
    yield
    for q in prob:
        q["d"] = jnp.where(head_lanes, q["y"] - q["mean"], 0.0)
        q["var"] = jnp.sum(q["d"] * q["d"], axis=1, keepdims=True) * inv_n
    yield
    for c in range(n_chunks):
        sl = slice(c * CHUNK, (c + 1) * CHUNK)
        for pr in range(n_pairs):
            ls = slice(pr * LANES, (pr + 1) * LANES)
            q = prob[c * n_pairs + pr]
            comb = q["d"] * lax.rsqrt(q["var"] + LNX_EPS) * lnx_w[:, ls] + q["bonus"]
            y = comb[0:CHUNK] + comb[CHUNK:n2] + lnx_b[:, ls]
            o_ref[sl, ls] = (y * g[sl, ls]).astype(o_ref.dtype)
    yield


def _mlstm_stages(p_ref, gb_ref, ng_ref, o_ref, cn_ref, m_ref, tb):
    t_idx = pl.program_id(1)

    @pl.when(t_idx == 0)
    def _():
        cn_ref[...] = jnp.zeros_like(cn_ref)
        m_ref[...] = jnp.zeros_like(m_ref)

    nqk = 2 * C_MQK
    q_all = p_ref[:, 0:C_MQK]
    k_all = p_ref[:, C_MQK:nqk]
    v_all = p_ref[:, nqk:nqk + C_MV]
    o_gate = p_ref[:, nqk + C_MV:nqk + 2 * C_MV]

    graw = p_ref[:, GATE_OFF:GATE_OFF + LANES] + gb_ref[...]
    capped = GATE_CAP * jnp.tanh(graw / GATE_CAP)
    logsig = jnp.minimum(capped, 0.0) - jnp.log(1.0 + jnp.exp(-jnp.abs(capped)))
    lane_g = lax.broadcasted_iota(jnp.int32, (tb, LANES), 1)
    gates = jnp.where(lane_g < H_M, capped, logsig)

    ig_sh = pltpu.roll(gates, H_M, axis=1)
    yield

    n2 = 2 * CHUNK
    tri = (lax.broadcasted_iota(jnp.int32, (CHUNK, CHUNK), 0)
           >= lax.broadcasted_iota(jnp.int32, (CHUNK, CHUNK), 1)).astype(BF16)
    lane_lo = lax.broadcasted_iota(jnp.int32, (CHUNK, LANES), 1) < DK_M
    causal_st = ((lax.broadcasted_iota(jnp.int32, (n2, CHUNK), 0) & (CHUNK - 1))
                 >= lax.broadcasted_iota(jnp.int32, (n2, CHUNK), 1))
    zpad = jnp.zeros((CHUNK, LANES), F32)
    ones_v = jnp.ones((CHUNK, DV_M), BF16)
    ones_sq = jnp.ones((DV_M, DV_M), BF16)
    row_sq = lax.broadcasted_iota(jnp.int32, (n2, LANES), 0)
    lane_sq = lax.broadcasted_iota(jnp.int32, (n2, LANES), 1)
    row_lo = row_sq < CHUNK
    n_chunks = tb // CHUNK
    n_pairs = H_M // 2
    gate_lane = [lane_sq == jnp.where(row_lo, H_M + 2 * pr, H_M + 2 * pr + 1) for pr in range(n_pairs)]
    spread = [(row_sq == jnp.where(lane_sq < DK_M, H_M + 2 * pr, H_M + 2 * pr + 1)).astype(BF16)
              for pr in range(n_pairs)]

    def stack2(x):
        return jnp.concatenate([jnp.where(lane_lo, x, 0.0), jnp.where(lane_lo, 0.0, x)], axis=0)

    def rep2(x):
        return jnp.concatenate([x, x], axis=0)

    def rows2(x0, x1, shape):
        return jnp.concatenate([jnp.broadcast_to(x0, shape), jnp.broadcast_to(x1, shape)], axis=0)

    prob = []
    tn = (((0,), (0,)), ((), ()))
    for c in range(n_chunks):
        sl = slice(c * CHUNK, (c + 1) * CHUNK)
        hi, lo = _split_hi_lo(gates[sl])
        cs = (jnp.dot(tri, hi, preferred_element_type=F32)
              + jnp.dot(tri, lo, preferred_element_type=F32))
        c_all = ig_sh[sl] - cs
        w_all = jnp.exp(c_all - jnp.max(c_all, axis=0, keepdims=True)).astype(BF16)
        c_t = jnp.concatenate([c_all, zpad], axis=0).T
        cs2 = rep2(cs)
        for pr in range(n_pairs):
            prob.append(dict(c=c, pr=pr, sl=sl, w_all=w_all, c_t=c_t, cs2=cs2))
    yield
    for q in prob:
        ls = slice(q["pr"] * LANES, (q["pr"] + 1) * LANES)
        q["q_st"] = stack2(q_all[q["sl"], ls]).astype(BF16)
        q["k_sl"] = k_all[q["sl"], ls]
        q["s_qk"] = lax.dot_general(q["q_st"], q["k_sl"].astype(BF16), (((1,), (1,)), ((), ())),
                                    preferred_element_type=F32)
    yield
    for q in prob:
        h0, h1 = 2 * q["pr"], 2 * q["pr"] + 1
        q["b_col"] = jnp.sum(jnp.where(gate_lane[q["pr"]], q["cs2"], 0.0), axis=1, keepdims=True)
        c_row = rows2(q["c_t"][H_M + h0:H_M + h0 + 1, 0:CHUNK], q["c_t"][H_M + h1:H_M + h1 + 1, 0:CHUNK],
                      (CHUNK, CHUNK))
        q["masked"] = jnp.where(causal_st, c_row, -jnp.inf)
        q["cm"] = jnp.max(q["masked"], axis=1, keepdims=True)
    yield
    for q in prob:
        h0, h1 = 2 * q["pr"], 2 * q["pr"] + 1
        q["s_d"] = (q["s_qk"] * jnp.exp(q["masked"] - q["cm"])).astype(BF16)
        q["v0"] = jnp.concatenate([v_all[q["sl"], h0 * DV_M:(h0 + 1) * DV_M].astype(BF16), ones_v], axis=1)
        q["v1"] = jnp.concatenate([v_all[q["sl"], h1 * DV_M:(h1 + 1) * DV_M].astype(BF16), ones_v], axis=1)
        q["kw"] = (q["k_sl"] * jnp.dot(q["w_all"], spread[q["pr"]], preferred_element_type=F32)).astype(BF16)
    yield
    for q in prob:
        q["sv"] = jnp.concatenate([jnp.dot(q["s_d"][0:CHUNK], q["v0"], preferred_element_type=F32),
                                   jnp.dot(q["s_d"][CHUNK:n2], q["v1"], preferred_element_type=F32)], axis=0)
        q["kv"] = jnp.where(jnp.concatenate([row_lo, row_lo], axis=1),
                            lax.dot_general(q["kw"], q["v0"], tn, preferred_element_type=F32),
                            lax.dot_general(q["kw"], q["v1"], tn, preferred_element_type=F32))
        q["b_last"] = rows2(q["b_col"][CHUNK - 1:CHUNK], q["b_col"][n2 - 1:n2], (CHUNK, 1))
        q["a_max"] = q["b_last"] + rows2(q["cm"][CHUNK - 1:CHUNK], q["cm"][n2 - 1:n2], (CHUNK, 1))
    yield

    cn_st = [cn_ref[i] for i in range(n_pairs)]
    m_st = [m_ref[i] for i in range(n_pairs)]
    for q in prob:
        pr = q["pr"]
        q["m_prev"] = m_st[pr]
        q["cn_prev"] = cn_st[pr].astype(BF16)
        m_new = jnp.maximum(q["b_last"] + q["m_prev"], q["a_max"])
        dec = jnp.exp(q["b_last"] + q["m_prev"] - m_new)
        beta = jnp.exp(q["a_max"] - m_new)
        cn_st[pr] = dec * cn_st[pr] + beta * q["kv"]
        m_st[pr] = m_new
    for pr in range(n_pairs):
        cn_ref[pr] = cn_st[pr]
        m_ref[pr] = m_st[pr]
    yield
    for q in prob:
        q["mx"] = jnp.maximum(q["cm"], q["m_prev"])
        q["qc"] = jnp.dot(q["q_st"], q["cn_prev"], preferred_element_type=F32)
    yield
    for q in prob:
        mix = jnp.exp(q["cm"] - q["mx"]) * q["sv"] + jnp.exp(q["m_prev"] - q["mx"]) * q["qc"]
        num = mix[:, 0:DV_M]
        den = mix[:, DV_M:2 * DV_M]
        q["hout"] = num / jnp.maximum(jnp.abs(den), jnp.exp(-(q["b_col"] + q["mx"])))
    yield
    for q in prob:
        q["ms"] = jnp.dot((q["hout"] * q["hout"]).astype(BF16), ones_sq,
                          preferred_element_type=F32) * (1.0 / DV_M)
    yield
    y_rows = []
    for c in range(n_chunks):
        y_heads = []
        for pr in range(n_pairs):
            q = prob[c * n_pairs + pr]
            hn = q["hout"] * lax.rsqrt(q["ms"] + NORM_EPS)
            y_heads.append(hn[0:CHUNK])
            y_heads.append(hn[CHUNK:n2])
        y_rows.append(jnp.concatenate(y_heads, axis=1))
    y = jnp.concatenate(y_rows, axis=0)
    o_ref[...] = (y * ng_ref[...] * _sigmoid(o_gate)).astype(o_ref.dtype)
    yield


RWKV_LEAD_STAGES = 5


def _timemix_kernel(prw_ref, vec_ref, wd_ref, wa_ref, wg_ref, pml_ref, gb_ref, ng_ref,
                    orw_ref, oml_ref, rstate_ref, cn_ref, m_ref, *, tb):
    rw = _rwkv_stages(prw_ref, vec_ref, wd_ref, wa_ref, wg_ref, orw_ref, rstate_ref, tb)
    ml = _mlstm_stages(pml_ref, gb_ref, ng_ref, oml_ref, cn_ref, m_ref, tb)
    for _ in range(RWKV_LEAD_STAGES):
        next(rw)
    live = [rw, ml]
    while live:
        for gen in list(live):
            if next(gen, StopIteration) is StopIteration:
                live.remove(gen)


def _timemix(p_rw, vecs, wd, wa, wg, p_ml, gb, ng, tb):
    bsz, t, _ = p_rw.shape
    const = lambda shape: pl.BlockSpec(shape, lambda b, i: (0,) * len(shape))
    blk = lambda width: pl.BlockSpec((None, tb, width), lambda b, i: (b, i, 0))
    return pl.pallas_call(
        functools.partial(_timemix_kernel, tb=tb),
        grid=(bsz, t // tb),
        in_specs=[
            blk(RW_COLS),
            const((SUBLANES, C_R)),
            const((LANES, C_R)),
            const((LANES, C_R)),
            const((GATE_LORA, C_R)),
            blk(ML_PAD),
            const((1, LANES)),
            const((1, C_MV)),
        ],
        out_specs=[blk(C_R), blk(C_MV)],
        out_shape=[jax.ShapeDtypeStruct((bsz, t, C_R), BF16),
                   jax.ShapeDtypeStruct((bsz, t, C_MV), BF16)],
        scratch_shapes=[
            pltpu.VMEM((H_R // 2, 2 * N_R, 2 * N_R), F32),
            pltpu.VMEM((H_M // 2, 2 * DK_M, 2 * DV_M), F32),
            pltpu.VMEM((H_M // 2, 2 * CHUNK, 1), F32),
        ],
        compiler_params=pltpu.CompilerParams(
            dimension_semantics=("parallel", "arbitrary"), vmem_limit_bytes=VMEM_LIMIT),
        name="timemix",
    )(p_rw, vecs, wd, wa, wg, p_ml, gb, ng)


def _ffn_kernel(x_ref, yr_ref, ym_ref, wo_ref, g2_ref, wup_ref, cw_ref, cb_ref, wdn_ref, gf_ref,
                o_ref, x1_ref, h2_ref, act_ref, carry_ref, *, tm, tiles_per_seq):
    i = pl.program_id(0)

    @pl.when(i % tiles_per_seq == 0)
    def _():
        carry_ref[...] = jnp.zeros_like(carry_ref)

    x1 = (x_ref[...]
          + jnp.dot(yr_ref[...], wo_ref[0:C_R, :], preferred_element_type=F32)
          + jnp.dot(ym_ref[...], wo_ref[C_R:C_R + C_MV, :], preferred_element_type=F32))
    x1_ref[...] = x1
    h2_ref[...] = _rmsnorm_rows(x1, g2_ref[...]).astype(BF16)

    fc = 256
    for j in range(D_FF // fc):
        cs = slice(j * fc, (j + 1) * fc)
        h2 = h2_ref[...]
        a = jnp.dot(h2, wup_ref[:, j * fc:(j + 1) * fc], preferred_element_type=F32)
        b = jnp.dot(h2, wup_ref[:, D_FF + j * fc:D_FF + (j + 1) * fc], preferred_element_type=F32)
        ext = jnp.concatenate([carry_ref[:, cs], a], axis=0)
        carry_ref[:, cs] = a[tm - SUBLANES:tm, :]
        conv = cb_ref[:, cs] + a * cw_ref[FFN_CONV - 1:FFN_CONV, cs]
        for s in range(1, FFN_CONV):
            sh = pltpu.roll(ext, s, axis=0)[SUBLANES:SUBLANES + tm]
            conv = conv + sh * cw_ref[FFN_CONV - 1 - s:FFN_CONV - s, cs]
        act_ref[:, cs] = (conv * _sigmoid(conv) * b).astype(BF16)

    nc = 256
    for j in range(D_MODEL // nc):
        cs = slice(j * nc, (j + 1) * nc)
        x1_ref[:, cs] = x1_ref[:, cs] + jnp.dot(act_ref[...], wdn_ref[:, cs],
                                                preferred_element_type=F32)
    o_ref[...] = _rmsnorm_rows(x1_ref[...], gf_ref[...])


def _ffn(x2d, y_rw, y_ml, w_out, g2, w_up, cw, cb, w_dn, gf, tm, seq):
    n = x2d.shape[0]
    const = lambda shape: pl.BlockSpec(shape, lambda i: (0,) * len(shape),
                                       pipeline_mode=pl.Buffered(1))
    return pl.pallas_call(
        functools.partial(_ffn_kernel, tm=tm, tiles_per_seq=seq // tm),
        grid=(n // tm,),
        in_specs=[
            pl.BlockSpec((tm, D_MODEL), lambda i: (i, 0)),
            pl.BlockSpec((tm, C_R), lambda i: (i, 0)),
            pl.BlockSpec((tm, C_MV), lambda i: (i, 0)),
            const((C_R + C_MV, D_MODEL)),
            const((1, D_MODEL)),
            const((D_MODEL, 2 * D_FF)),
            const((FFN_CONV, D_FF)),
            const((1, D_FF)),
            const((D_FF, D_MODEL)),
            const((1, D_MODEL)),
        ],
        out_specs=pl.BlockSpec((tm, D_MODEL), lambda i: (i, 0)),
        out_shape=jax.ShapeDtypeStruct((n, D_MODEL), F32),
        scratch_shapes=[
            pltpu.VMEM((tm, D_MODEL), F32),
            pltpu.VMEM((tm, D_MODEL), BF16),
            pltpu.VMEM((tm, D_FF), BF16),
            pltpu.VMEM((SUBLANES, D_FF), F32),
        ],
        compiler_params=pltpu.CompilerParams(
            dimension_semantics=("arbitrary",), vmem_limit_bytes=VMEM_LIMIT),
        name="outproj_convffn",
    )(x2d, y_rw, y_ml, w_out, g2, w_up, cw, cb, w_dn, gf)


def _row(v):
    return v.reshape(1, -1)


def _layer(x, norm1_g, w_in, rw_mu, w0, w_up_decay, a0, w_up_a, w_up_g, k_k, k_a, r_k,
           lnx_w, lnx_b, qk_conv_w, qk_conv_b, i_bias, f_bias, mh_norm_g, w_out,
           norm2_g, w_ffn_up, ffn_conv_w, ffn_conv_b, w_ffn_down, out_g):
    bsz, seq, _ = x.shape
    n = bsz * seq
    tm = min(512, seq)
    tb = min(256, seq)
    x2d = x.reshape(n, D_MODEL)

    w_in_p = jnp.pad(w_in, ((0, 0), (0, ML_PAD - ML_COLS))).astype(BF16)
    p_rw, p_ml = _inproj(x2d, _row(norm1_g), w_in_p, _row(rw_mu), qk_conv_w, _row(qk_conv_b), tm, seq)

    zeros_lora = jnp.zeros((DECAY_LORA, C_R), F32)
    wd = jnp.concatenate([w_up_decay, zeros_lora], axis=0).astype(BF16)
    wa = jnp.concatenate([zeros_lora, w_up_a], axis=0).astype(BF16)
    vecs = jnp.stack([w0, a0, k_k, k_a, r_k.reshape(C_R), lnx_w, lnx_b, jnp.zeros_like(w0)], axis=0)
    gate_bias = jnp.pad(jnp.concatenate([i_bias, f_bias]), (0, LANES - 2 * H_M))
    y_rw, y_ml = _timemix(p_rw.reshape(bsz, seq, RW_COLS), vecs, wd, wa, w_up_g.astype(BF16),
                          p_ml.reshape(bsz, seq, ML_PAD), _row(gate_bias), _row(mh_norm_g), tb)

    out = _ffn(x2d, y_rw.reshape(n, C_R), y_ml.reshape(n, C_MV), w_out.astype(BF16), _row(norm2_g),
               w_ffn_up.astype(BF16), ffn_conv_w, _row(ffn_conv_b), w_ffn_down.astype(BF16),
               _row(out_g), tm, seq)
    return out.reshape(bsz, seq, D_MODEL)


def kernel(x, norm1_g, w_in, rw_mu, w0, w_up_decay, a0, w_up_a, w_up_g, k_k, k_a, r_k, lnx_w, lnx_b,
           qk_conv_w, qk_conv_b, i_bias, f_bias, mh_norm_g, w_out, norm2_g, w_ffn_up, ffn_conv_w,
           ffn_conv_b, w_ffn_down, norm_f_g):
    assert norm1_g.shape[0] == 1
    return _layer(x, norm1_g[0], w_in[0], rw_mu[0], w0[0], w_up_decay[0], a0[0], w_up_a[0],
                  w_up_g[0], k_k[0], k_a[0], r_k[0], lnx_w[0], lnx_b[0], qk_conv_w[0],
                  qk_conv_b[0], i_bias[0], f_bias[0], mh_norm_g[0], w_out[0], norm2_g[0],
                  w_ffn_up[0], ffn_conv_w[0], ffn_conv_b[0], w_ffn_down[0], norm_f_g)
```

```python
import functools
import math

import jax
import jax.numpy as jnp
from jax import lax
from jax.experimental import pallas as pl
from jax.experimental.pallas import tpu as pltpu

F32 = jnp.float32
BF16 = jnp.bfloat16

D_MODEL = 1024
H_R, N_R = 8, 64
C_R = H_R * N_R
DECAY_LORA, AAA_LORA, GATE_LORA = 64, 64, 128
RW_COLS = 3 * C_R + DECAY_LORA + AAA_LORA + GATE_LORA
LNX_EPS = 64e-5
H_M, DK_M, DV_M = 4, 64, 128
C_MQK = H_M * DK_M
C_MV = H_M * DV_M
ML_COLS = 2 * C_MQK + 2 * C_MV + 2 * H_M
QK_CONV = 4
GATE_CAP = 15.0
D_FF = 2816
FFN_CONV = 3
NORM_EPS = 1e-6
CHUNK = 64

LANES = 128
SUBLANES = 8
BF16_ROWS = 16
ML_PAD = 1664
GATE_OFF = 2 * C_MQK + 2 * C_MV
VMEM_LIMIT = 56 * 1024 * 1024

EXP_M05 = math.exp(-0.5)


def _dot(a, b):
    return jnp.dot(a.astype(BF16), b.astype(BF16), preferred_element_type=F32)


def _split_hi_lo(x):
    hi = x.astype(BF16)
    lo = (x - hi.astype(F32)).astype(BF16)
    return hi, lo


def _sigmoid(x):
    return 1.0 / (1.0 + jnp.exp(-x))


def _rmsnorm_rows(x, g):
    ms = jnp.mean(x * x, axis=-1, keepdims=True)
    return x * lax.rsqrt(ms + NORM_EPS) * g


def _inproj_kernel(x_ref, g_ref, w_ref, mu_ref, cw_ref, cb_ref, orw_ref, oml_ref,
                   rcarry_ref, mcarry_ref, *, tm, tiles_per_seq):
    i = pl.program_id(0)

    @pl.when(i % tiles_per_seq == 0)
    def _():
        rcarry_ref[...] = jnp.zeros_like(rcarry_ref)
        mcarry_ref[...] = jnp.zeros_like(mcarry_ref)

    h = _rmsnorm_rows(x_ref[...], g_ref[...]).astype(BF16)
    row0 = lax.broadcasted_iota(jnp.int32, (tm, 1), 0) == 0
    step = 256
    for c in range(0, RW_COLS, step):
        cs = slice(c, c + step)
        p = jnp.dot(h, w_ref[:, cs], preferred_element_type=F32)
        shifted = jnp.where(row0, rcarry_ref[SUBLANES - 1:SUBLANES, cs], pltpu.roll(p, 1, axis=0))
        rcarry_ref[:, cs] = p[tm - SUBLANES:tm, :]
        orw_ref[:, cs] = p + (shifted - p) * mu_ref[:, cs]
    nqk = 2 * C_MQK
    for c in range(0, nqk, step):
        cs = slice(c, c + step)
        p = jnp.dot(h, w_ref[:, RW_COLS + c:RW_COLS + c + step], preferred_element_type=F32)
        ext = jnp.concatenate([mcarry_ref[:, cs], p], axis=0)
        mcarry_ref[:, cs] = p[tm - SUBLANES:tm, :]
        acc = cb_ref[:, cs] + p * cw_ref[QK_CONV - 1:QK_CONV, cs]
        for s in range(1, QK_CONV):
            sh = pltpu.roll(ext, s, axis=0)[SUBLANES:SUBLANES + tm]
            acc = acc + sh * cw_ref[QK_CONV - 1 - s:QK_CONV - s, cs]
        qk = acc * _sigmoid(acc)
        if c < C_MQK:
            qk = qk * (DK_M ** -0.5)
        oml_ref[:, cs] = qk
    for c in range(nqk, ML_PAD, step):
        n = min(step, ML_PAD - c)
        oml_ref[:, c:c + n] = jnp.dot(h, w_ref[:, RW_COLS + c:RW_COLS + c + n],
                                      preferred_element_type=F32)


def _inproj(x2d, g1, w_in_p, mu, cw, cb, tm, seq):
    n = x2d.shape[0]
    const = lambda shape: pl.BlockSpec(shape, lambda i: (0,) * len(shape))
    return pl.pallas_call(
        functools.partial(_inproj_kernel, tm=tm, tiles_per_seq=seq // tm),
        grid=(n // tm,),
        in_specs=[
            pl.BlockSpec((tm, D_MODEL), lambda i: (i, 0)),
            const((1, D_MODEL)),
            const((D_MODEL, RW_COLS + ML_PAD)),
            const((1, RW_COLS)),
            const((QK_CONV, 2 * C_MQK)),
            const((1, 2 * C_MQK)),
        ],
        out_specs=[
            pl.BlockSpec((tm, RW_COLS), lambda i: (i, 0)),
            pl.BlockSpec((tm, ML_PAD), lambda i: (i, 0)),
        ],
        out_shape=[
            jax.ShapeDtypeStruct((n, RW_COLS), F32),
            jax.ShapeDtypeStruct((n, ML_PAD), F32),
        ],
        scratch_shapes=[
            pltpu.VMEM((SUBLANES, RW_COLS), F32),
            pltpu.VMEM((SUBLANES, 2 * C_MQK), F32),
        ],
        compiler_params=pltpu.CompilerParams(
            dimension_semantics=("arbitrary",), vmem_limit_bytes=VMEM_LIMIT),
        name="inproj",
    )(x2d, g1, w_in_p, mu, cw, cb)


def _rwkv_stages(p_ref, vec_ref, wd_ref, wa_ref, wg_ref, o_ref, state_ref, tb):
    t_idx = pl.program_id(1)

    @pl.when(t_idx == 0)
    def _():
        state_ref[...] = jnp.zeros_like(state_ref)

    w0 = vec_ref[0:1, :]
    a0 = vec_ref[1:2, :]
    k_k = vec_ref[2:3, :]
    k_a = vec_ref[3:4, :]
    r_k = vec_ref[4:5, :]
    lnx_w = vec_ref[5:6, :]
    lnx_b = vec_ref[6:7, :]

    r = p_ref[:, 0:C_R]
    k = p_ref[:, C_R:2 * C_R]
    v = p_ref[:, 2 * C_R:3 * C_R]
    z = p_ref[:, 3 * C_R:3 * C_R + LANES]
    g_lo = p_ref[:, 3 * C_R + LANES:RW_COLS]

    u = w0 + _dot(jnp.tanh(z), wd_ref[...])
    ld = -EXP_M05 * _sigmoid(u)
    a = _sigmoid(a0 + _dot(z, wa_ref[...]))
    g = _dot(_sigmoid(g_lo), wg_ref[...])
    kk = k * k_k
    kmod = k * (1.0 + (a - 1.0) * k_a)
    rkr = r * kmod * r_k
    yield

    n2 = 2 * CHUNK
    ri = lax.broadcasted_iota(jnp.int32, (n2, n2), 0)
    ci = lax.broadcasted_iota(jnp.int32, (n2, n2), 1)
    strict_lower = ri > ci
    lower = ri >= ci
    eye = (ri == ci).astype(F32).astype(BF16)
    head_lanes = (ri < CHUNK) == (ci < N_R)
    lvl_sel = []
    for l in range(6):
        m = (((ri >> (l + 1)) == (ci >> (l + 1))) & (((ri >> l) & 1) == 1) & (((ci >> l) & 1) == 0))
        lvl_sel.append(m.astype(F32).astype(BF16))
    lane_lo = lax.broadcasted_iota(jnp.int32, (CHUNK, LANES), 1) < N_R
    tri = (lax.broadcasted_iota(jnp.int32, (CHUNK, CHUNK), 0)
           >= lax.broadcasted_iota(jnp.int32, (CHUNK, CHUNK), 1)).astype(BF16)

    def stack2(x):
        return jnp.concatenate([jnp.where(lane_lo, x, 0.0), jnp.where(lane_lo, 0.0, x)], axis=0)

    def rep2(x):
        return jnp.concatenate([x, x], axis=0)

    n_chunks = tb // CHUNK
    n_pairs = H_R // 2
    zeros_sq = jnp.zeros((n2, n2), BF16)
    inv_n = 1.0 / N_R

    prob = []
    for c in range(n_chunks):
        sl = slice(c * CHUNK, (c + 1) * CHUNK)
        ld_c = ld[sl]
        hi, lo = _split_hi_lo(ld_c)
        cs = (jnp.dot(tri, hi, preferred_element_type=F32)
              + jnp.dot(tri, lo, preferred_element_type=F32))
        g_in = jnp.exp(cs)
        g_ex = jnp.exp(cs - ld_c)
        g_inv = jnp.exp(-cs)
        rt = r[sl] * g_in
        kt = kmod[sl] * g_inv
        ag = a[sl] * g_inv
        g_last = g_in[CHUNK - 1:CHUNK, :]
        for pr in range(n_pairs):
            ls = slice(pr * LANES, (pr + 1) * LANES)
            kk_st = stack2(kk[sl, ls])
            kkn_st = kk_st * jnp.minimum(lax.rsqrt(jnp.sum(kk_st * kk_st, axis=1, keepdims=True)), 1e12)
            p_a = (-kkn_st * rep2(g_ex[:, ls])).astype(BF16)
            p_b = (kkn_st * rep2(ag[:, ls])).astype(BF16)
            p_r = stack2(rt[:, ls]).astype(BF16)
            p_k = stack2(kt[:, ls]).astype(BF16)
            v_f = stack2(v[sl, ls])
            v_st = v_f.astype(BF16)
            bonus = jnp.sum(stack2(rkr[sl, ls]), axis=1, keepdims=True) * v_f
            p_bk = jnp.concatenate([p_b, p_k], axis=0)
            gram = lax.dot_general(jnp.concatenate([p_a, p_r], axis=0), p_bk,
                                   (((1,), (1,)), ((), ())), preferred_element_type=F32)
            g_ab = gram[0:n2, 0:n2].astype(BF16)
            gl_col = jnp.broadcast_to(g_last[:, ls], (n2, n2)).T
            prob.append(dict(
                p_a=p_a, p_r=p_r, p_bk=p_bk, v_st=v_st, g_ab=g_ab, bonus=bonus,
                a_ak=jnp.where(strict_lower, gram[0:n2, n2:2 * n2], 0.0).astype(BF16),
                a_rb=jnp.where(lower, gram[n2:2 * n2, 0:n2], 0.0).astype(BF16),
                a_rk=jnp.where(lower, gram[n2:2 * n2, n2:2 * n2], 0.0).astype(BF16),
                x=eye + g_ab * lvl_sel[0],
                g_last=gl_col))
        yield
    def odd_blocks(x, s):
        return jnp.concatenate([x[i * s:(i + 1) * s] for i in range(1, n2 // s, 2)], axis=0)

    def add_odd_blocks(x, u, s):
        parts = []
        for i in range(n2 // s):
            blk = x[i * s:(i + 1) * s]
            parts.append(blk + u[(i // 2) * s:(i // 2 + 1) * s] if i % 2 else blk)
        return jnp.concatenate(parts, axis=0)

    for l in range(1, 6):
        s = 1 << l
        sub = s >= BF16_ROWS
        for q in prob:
            lhs = odd_blocks(q["x"], s) if sub else q["x"]
            q["t"] = jnp.dot(lhs, q["g_ab"] * lvl_sel[l], preferred_element_type=F32).astype(BF16)
        yield
        for q in prob:
            u = jnp.dot(q["t"], q["x"], preferred_element_type=F32).astype(BF16)
            q["x"] = add_odd_blocks(q["x"], u, s) if sub else q["x"] + u
        yield
    for q in prob:
        s1 = jnp.dot(jnp.concatenate([q["a_ak"], q["a_rk"]], axis=0), q["v_st"],
                     preferred_element_type=F32)
        q["wv"] = s1[0:n2].astype(BF16)
        q["arkv"] = s1[n2:2 * n2]
    yield
    for q in prob:
        q["xc"] = jnp.dot(q["x"], jnp.concatenate([q["wv"], q["p_a"]], axis=1),
                          preferred_element_type=F32).astype(BF16)
    yield
    for q in prob:
        s2 = jnp.dot(q["a_rb"], q["xc"], preferred_element_type=F32)
        q["yv"] = s2[:, 0:n2] + q["arkv"]
        r_eff = q["p_r"].astype(F32) + s2[:, n2:2 * n2]
        rhs = jnp.concatenate([q["xc"], jnp.concatenate([q["v_st"], zeros_sq], axis=1)], axis=0)
        hb = lax.dot_general(q["p_bk"], rhs, (((0,), (0,)), ((), ())),
                             preferred_element_type=F32)
        q["h"] = hb[:, 0:n2]
        q["lhs"] = jnp.concatenate([r_eff.astype(BF16), hb[:, n2:2 * n2].astype(BF16)], axis=0)
    yield

    states = [state_ref[i] for i in range(n_pairs)]
    for c in range(n_chunks):
        for pr in range(n_pairs):
            q = prob[c * n_pairs + pr]
            m0 = states[pr]
            o2 = jnp.dot(q["lhs"], m0.astype(BF16), preferred_element_type=F32)
            states[pr] = q["g_last"] * (m0 + o2[n2:2 * n2] + q["h"])
            q["y"] = o2[0:n2] + q["yv"]
        yield
    for pr in range(n_pairs):
        state_ref[pr] = states[pr]
    for q in prob:
        q["mean"] = jnp.sum(q["y"], axis=1, keepdims=True) * inv_n
    yield
    for q in prob:
        q["d"] = jnp.where(head_lanes, q["y"] - q["mean"], 0.0)
        q["var"] = jnp.sum(q["d"] * q["d"], axis=1, keepdims=True) * inv_n
    yield
    for c in range(n_chunks):
        sl = slice(c * CHUNK, (c + 1) * CHUNK)
        for pr in range(n_pairs):
            ls = slice(pr * LANES, (pr + 1) * LANES)
            q = prob[c * n_pairs + pr]
            comb = q["d"] * lax.rsqrt(q["var"] + LNX_EPS) * lnx_w[:, ls] + q["bonus"]
            y = comb[0:CHUNK] + comb[CHUNK:n2] + lnx_b[:, ls]
            o_ref[sl, ls] = (y * g[sl, ls]).astype(o_ref.dtype)
    yield


def _mlstm_stages(p_ref, gb_ref, ng_ref, o_ref, cn_ref, m_ref, tb):
    t_idx = pl.program_id(1)

    @pl.when(t_idx == 0)
    def _():
        cn_ref[...] = jnp.zeros_like(cn_ref)
        m_ref[...] = jnp.zeros_like(m_ref)

    nqk = 2 * C_MQK
    q_all = p_ref[:, 0:C_MQK]
    k_all = p_ref[:, C_MQK:nqk]
    v_all = p_ref[:, nqk:nqk + C_MV]
    o_gate = p_ref[:, nqk + C_MV:nqk + 2 * C_MV]

    graw = p_ref[:, GATE_OFF:GATE_OFF + LANES] + gb_ref[...]
    capped = GATE_CAP * jnp.tanh(graw / GATE_CAP)
    logsig = jnp.minimum(capped, 0.0) - jnp.log(1.0 + jnp.exp(-jnp.abs(capped)))
    lane_g = lax.broadcasted_iota(jnp.int32, (tb, LANES), 1)
    gates = jnp.where(lane_g < H_M, capped, logsig)

    ig_sh = pltpu.roll(gates, H_M, axis=1)
    yield

    n2 = 2 * CHUNK
    tri = (lax.broadcasted_iota(jnp.int32, (CHUNK, CHUNK), 0)
           >= lax.broadcasted_iota(jnp.int32, (CHUNK, CHUNK), 1)).astype(BF16)
    lane_lo = lax.broadcasted_iota(jnp.int32, (CHUNK, LANES), 1) < DK_M
    causal_st = ((lax.broadcasted_iota(jnp.int32, (n2, CHUNK), 0) & (CHUNK - 1))
                 >= lax.broadcasted_iota(jnp.int32, (n2, CHUNK), 1))
    zpad = jnp.zeros((CHUNK, LANES), F32)
    ones_v = jnp.ones((CHUNK, DV_M), BF16)
    ones_sq = jnp.ones((DV_M, DV_M), BF16)
    row_sq = lax.broadcasted_iota(jnp.int32, (n2, LANES), 0)
    lane_sq = lax.broadcasted_iota(jnp.int32, (n2, LANES), 1)
    row_lo = row_sq < CHUNK
    n_chunks = tb // CHUNK
    n_pairs = H_M // 2
    gate_lane = [lane_sq == jnp.where(row_lo, H_M + 2 * pr, H_M + 2 * pr + 1) for pr in range(n_pairs)]
    spread = [(row_sq == jnp.where(lane_sq < DK_M, H_M + 2 * pr, H_M + 2 * pr + 1)).astype(BF16)
              for pr in range(n_pairs)]

    def stack2(x):
        return jnp.concatenate([jnp.where(lane_lo, x, 0.0), jnp.where(lane_lo, 0.0, x)], axis=0)

    def rep2(x):
        return jnp.concatenate([x, x], axis=0)

    def rows2(x0, x1, shape):
        return jnp.concatenate([jnp.broadcast_to(x0, shape), jnp.broadcast_to(x1, shape)], axis=0)

    prob = []
    tn = (((0,), (0,)), ((), ()))
    for c in range(n_chunks):
        sl = slice(c * CHUNK, (c + 1) * CHUNK)
        hi, lo = _split_hi_lo(gates[sl])
        cs = (jnp.dot(tri, hi, preferred_element_type=F32)
              + jnp.dot(tri, lo, preferred_element_type=F32))
        c_all = ig_sh[sl] - cs
        w_all = jnp.exp(c_all - jnp.max(c_all, axis=0, keepdims=True)).astype(BF16)
        c_t = jnp.concatenate([c_all, zpad], axis=0).T
        cs2 = rep2(cs)
        for pr in range(n_pairs):
            prob.append(dict(c=c, pr=pr, sl=sl, w_all=w_all, c_t=c_t, cs2=cs2))
    yield
    for q in prob:
        ls = slice(q["pr"] * LANES, (q["pr"] + 1) * LANES)
        q["q_st"] = stack2(q_all[q["sl"], ls]).astype(BF16)
        q["k_sl"] = k_all[q["sl"], ls]
        q["s_qk"] = lax.dot_general(q["q_st"], q["k_sl"].astype(BF16), (((1,), (1,)), ((), ())),
                                    preferred_element_type=F32)
    yield
    for q in prob:
        h0, h1 = 2 * q["pr"], 2 * q["pr"] + 1
        q["b_col"] = jnp.sum(jnp.where(gate_lane[q["pr"]], q["cs2"], 0.0), axis=1, keepdims=True)
        c_row = rows2(q["c_t"][H_M + h0:H_M + h0 + 1, 0:CHUNK], q["c_t"][H_M + h1:H_M + h1 + 1, 0:CHUNK],
                      (CHUNK, CHUNK))
        q["masked"] = jnp.where(causal_st, c_row, -jnp.inf)
        q["cm"] = jnp.max(q["masked"], axis=1, keepdims=True)
    yield
    for q in prob:
        h0, h1 = 2 * q["pr"], 2 * q["pr"] + 1
        q["s_d"] = (q["s_qk"] * jnp.exp(q["masked"] - q["cm"])).astype(BF16)
        q["v0"] = jnp.concatenate([v_all[q["sl"], h0 * DV_M:(h0 + 1) * DV_M].astype(BF16), ones_v], axis=1)
        q["v1"] = jnp.concatenate([v_all[q["sl"], h1 * DV_M:(h1 + 1) * DV_M].astype(BF16), ones_v], axis=1)
        q["kw"] = (q["k_sl"] * jnp.dot(q["w_all"], spread[q["pr"]], preferred_element_type=F32)).astype(BF16)
    yield
    for q in prob:
        q["sv"] = jnp.concatenate([jnp.dot(q["s_d"][0:CHUNK], q["v0"], preferred_element_type=F32),
                                   jnp.dot(q["s_d"][CHUNK:n2], q["v1"], preferred_element_type=F32)], axis=0)
        q["kv"] = jnp.where(jnp.concatenate([row_lo, row_lo], axis=1),
                            lax.dot_general(q["kw"], q["v0"], tn, preferred_element_type=F32),
                            lax.dot_general(q["kw"], q["v1"], tn, preferred_element_type=F32))
        q["b_last"] = rows2(q["b_col"][CHUNK - 1:CHUNK], q["b_col"][n2 - 1:n2], (CHUNK, 1))
        q["a_max"] = q["b_last"] + rows2(q["cm"][CHUNK - 1:CHUNK], q["cm"][n2 - 1:n2], (CHUNK, 1))
    yield

    cn_st = [cn_ref[i] for i in range(n_pairs)]
    m_st = [m_ref[i] for i in range(n_pairs)]
    for q in prob:
        pr = q["pr"]
        q["m_prev"] = m_st[pr]
        q["cn_prev"] = cn_st[pr].astype(BF16)
        m_new = jnp.maximum(q["b_last"] + q["m_prev"], q["a_max"])
        dec = jnp.exp(q["b_last"] + q["m_prev"] - m_new)
        beta = jnp.exp(q["a_max"] - m_new)
        cn_st[pr] = dec * cn_st[pr] + beta * q["kv"]
        m_st[pr] = m_new
    for pr in range(n_pairs):
        cn_ref[pr] = cn_st[pr]
        m_ref[pr] = m_st[pr]
    yield
    for q in prob:
        q["mx"] = jnp.maximum(q["cm"], q["m_prev"])
        q["qc"] = jnp.dot(q["q_st"], q["cn_prev"], preferred_element_type=F32)
    yield
    for q in prob:
        mix = jnp.exp(q["cm"] - q["mx"]) * q["sv"] + jnp.exp(q["m_prev"] - q["mx"]) * q["qc"]
        num = mix[:, 0:DV_M]
        den = mix[:, DV_M:2 * DV_M]
        q["hout"] = num / jnp.maximum(jnp.abs(den), jnp.exp(-(q["b_col"] + q["mx"])))
    yield
    for q in prob:
        q["ms"] = jnp.dot((q["hout"] * q["hout"]).astype(BF16), ones_sq,
                          preferred_element_type=F32) * (1.0 / DV_M)
    yield
    y_rows = []
    for c in range(n_chunks):
        y_heads = []
        for pr in range(n_pairs):
            q = prob[c * n_pairs + pr]
            hn = q["hout"] * lax.rsqrt(q["ms"] + NORM_EPS)
            y_heads.append(hn[0:CHUNK])
            y_heads.append(hn[CHUNK:n2])
        y_rows.append(jnp.concatenate(y_heads, axis=1))
    y = jnp.concatenate(y_rows, axis=0)
    o_ref[...] = (y * ng_ref[...] * _sigmoid(o_gate)).astype(o_ref.dtype)
    yield


RWKV_LEAD_STAGES = 9
RWKV_STAGES_PER_MLSTM = 2


def _timemix_kernel(prw_ref, vec_ref, wd_ref, wa_ref, wg_ref, pml_ref, gb_ref, ng_ref,
                    orw_ref, oml_ref, rstate_ref, cn_ref, m_ref, *, tb):
    rw = _rwkv_stages(prw_ref, vec_ref, wd_ref, wa_ref, wg_ref, orw_ref, rstate_ref, tb)
    ml = _mlstm_stages(pml_ref, gb_ref, ng_ref, oml_ref, cn_ref, m_ref, tb)
    for _ in range(RWKV_LEAD_STAGES):
        next(rw)
    order = [rw] * RWKV_STAGES_PER_MLSTM + [ml]
    done = set()
    while len(done) < 2:
        for gen in order:
            if id(gen) not in done and next(gen, StopIteration) is StopIteration:
                done.add(id(gen))


def _timemix(p_rw, vecs, wd, wa, wg, p_ml, gb, ng, tb):
    bsz, t, _ = p_rw.shape
    const = lambda shape: pl.BlockSpec(shape, lambda b, i: (0,) * len(shape))
    blk = lambda width: pl.BlockSpec((None, tb, width), lambda b, i: (b, i, 0))
    return pl.pallas_call(
        functools.partial(_timemix_kernel, tb=tb),
        grid=(bsz, t // tb),
        in_specs=[
            blk(RW_COLS),
            const((SUBLANES, C_R)),
            const((LANES, C_R)),
            const((LANES, C_R)),
            const((GATE_LORA, C_R)),
            blk(ML_PAD),
            const((1, LANES)),
            const((1, C_MV)),
        ],
        out_specs=[blk(C_R), blk(C_MV)],
        out_shape=[jax.ShapeDtypeStruct((bsz, t, C_R), BF16),
                   jax.ShapeDtypeStruct((bsz, t, C_MV), BF16)],
        scratch_shapes=[
            pltpu.VMEM((H_R // 2, 2 * N_R, 2 * N_R), F32),
            pltpu.VMEM((H_M // 2, 2 * DK_M, 2 * DV_M), F32),
            pltpu.VMEM((H_M // 2, 2 * CHUNK, 1), F32),
        ],
        compiler_params=pltpu.CompilerParams(
            dimension_semantics=("parallel", "arbitrary"), vmem_limit_bytes=VMEM_LIMIT),
        name="timemix",
    )(p_rw, vecs, wd, wa, wg, p_ml, gb, ng)


def _ffn_kernel(x_ref, yr_ref, ym_ref, wo_ref, g2_ref, wup_ref, cw_ref, cb_ref, wdn_ref, gf_ref,
                o_ref, x1_ref, h2_ref, act_ref, carry_ref, *, tm, tiles_per_seq):
    i = pl.program_id(0)

    @pl.when(i % tiles_per_seq == 0)
    def _():
        carry_ref[...] = jnp.zeros_like(carry_ref)

    x1 = (x_ref[...]
          + jnp.dot(yr_ref[...], wo_ref[0:C_R, :], preferred_element_type=F32)
          + jnp.dot(ym_ref[...], wo_ref[C_R:C_R + C_MV, :], preferred_element_type=F32))
    x1_ref[...] = x1
    h2_ref[...] = _rmsnorm_rows(x1, g2_ref[...]).astype(BF16)

    fc = 256
    for j in range(D_FF // fc):
        cs = slice(j * fc, (j + 1) * fc)
        h2 = h2_ref[...]
        a = jnp.dot(h2, wup_ref[:, j * fc:(j + 1) * fc], preferred_element_type=F32)
        b = jnp.dot(h2, wup_ref[:, D_FF + j * fc:D_FF + (j + 1) * fc], preferred_element_type=F32)
        ext = jnp.concatenate([carry_ref[:, cs], a], axis=0)
        carry_ref[:, cs] = a[tm - SUBLANES:tm, :]
        conv = cb_ref[:, cs] + a * cw_ref[FFN_CONV - 1:FFN_CONV, cs]
        for s in range(1, FFN_CONV):
            sh = pltpu.roll(ext, s, axis=0)[SUBLANES:SUBLANES + tm]
            conv = conv + sh * cw_ref[FFN_CONV - 1 - s:FFN_CONV - s, cs]
        act_ref[:, cs] = (conv * _sigmoid(conv) * b).astype(BF16)

    nc = 256
    for j in range(D_MODEL // nc):
        cs = slice(j * nc, (j + 1) * nc)
        x1_ref[:, cs] = x1_ref[:, cs] + jnp.dot(act_ref[...], wdn_ref[:, cs],
                                                preferred_element_type=F32)
    o_ref[...] = _rmsnorm_rows(x1_ref[...], gf_ref[...])


def _ffn(x2d, y_rw, y_ml, w_out, g2, w_up, cw, cb, w_dn, gf, tm, seq):
    n = x2d.shape[0]
    const = lambda shape: pl.BlockSpec(shape, lambda i: (0,) * len(shape),
                                       pipeline_mode=pl.Buffered(1))
    return pl.pallas_call(
        functools.partial(_ffn_kernel, tm=tm, tiles_per_seq=seq // tm),
        grid=(n // tm,),
        in_specs=[
            pl.BlockSpec((tm, D_MODEL), lambda i: (i, 0)),
            pl.BlockSpec((tm, C_R), lambda i: (i, 0)),
            pl.BlockSpec((tm, C_MV), lambda i: (i, 0)),
            const((C_R + C_MV, D_MODEL)),
            const((1, D_MODEL)),
            const((D_MODEL, 2 * D_FF)),
            const((FFN_CONV, D_FF)),
            const((1, D_FF)),
            const((D_FF, D_MODEL)),
            const((1, D_MODEL)),
        ],
        out_specs=pl.BlockSpec((tm, D_MODEL), lambda i: (i, 0)),
        out_shape=jax.ShapeDtypeStruct((n, D_MODEL), F32),
        scratch_shapes=[
            pltpu.VMEM((tm, D_MODEL), F32),
            pltpu.VMEM((tm, D_MODEL), BF16),
            pltpu.VMEM((tm, D_FF), BF16),
            pltpu.VMEM((SUBLANES, D_FF), F32),
        ],
        compiler_params=pltpu.CompilerParams(
            dimension_semantics=("arbitrary",), vmem_limit_bytes=VMEM_LIMIT),
        name="outproj_convffn",
    )(x2d, y_rw, y_ml, w_out, g2, w_up, cw, cb, w_dn, gf)


def _row(v):
    return v.reshape(1, -1)


def _layer(x, norm1_g, w_in, rw_mu, w0, w_up_decay, a0, w_up_a, w_up_g, k_k, k_a, r_k,
           lnx_w, lnx_b, qk_conv_w, qk_conv_b, i_bias, f_bias, mh_norm_g, w_out,
           norm2_g, w_ffn_up, ffn_conv_w, ffn_conv_b, w_ffn_down, out_g):
    bsz, seq, _ = x.shape
    n = bsz * seq
    tm = min(1024, seq)
    tb = min(256, seq)
    x2d = x.reshape(n, D_MODEL)

    w_in_p = jnp.pad(w_in, ((0, 0), (0, ML_PAD - ML_COLS))).astype(BF16)
    p_rw, p_ml = _inproj(x2d, _row(norm1_g), w_in_p, _row(rw_mu), qk_conv_w, _row(qk_conv_b), tm, seq)

    zeros_lora = jnp.zeros((DECAY_LORA, C_R), F32)
    wd = jnp.concatenate([w_up_decay, zeros_lora], axis=0).astype(BF16)
    wa = jnp.concatenate([zeros_lora, w_up_a], axis=0).astype(BF16)
    vecs = jnp.stack([w0, a0, k_k, k_a, r_k.reshape(C_R), lnx_w, lnx_b, jnp.zeros_like(w0)], axis=0)
    gate_bias = jnp.pad(jnp.concatenate([i_bias, f_bias]), (0, LANES - 2 * H_M))
    y_rw, y_ml = _timemix(p_rw.reshape(bsz, seq, RW_COLS), vecs, wd, wa, w_up_g.astype(BF16),
                          p_ml.reshape(bsz, seq, ML_PAD), _row(gate_bias), _row(mh_norm_g), tb)

    out = _ffn(x2d, y_rw.reshape(n, C_R), y_ml.reshape(n, C_MV), w_out.astype(BF16), _row(norm2_g),
               w_ffn_up.astype(BF16), ffn_conv_w, _row(ffn_conv_b), w_ffn_down.astype(BF16),
               _row(out_g), tm, seq)
    return out.reshape(bsz, seq, D_MODEL)


def kernel(x, norm1_g, w_in, rw_mu, w0, w_up_decay, a0, w_up_a, w_up_g, k_k, k_a, r_k, lnx_w, lnx_b,
           qk_conv_w, qk_conv_b, i_bias, f_bias, mh_norm_g, w_out, norm2_g, w_ffn_up, ffn_conv_w,
           ffn_conv_b, w_ffn_down, norm_f_g):
    assert norm1_g.shape[0] == 1
    return _layer(x, norm1_g[0], w_in[0], rw_mu[0], w0[0], w_up_decay[0], a0[0], w_up_a[0],
                  w_up_g[0], k_k[0], k_a[0], r_k[0], lnx_w[0], lnx_b[0], qk_conv_w[0],
                  qk_conv_b[0], i_bias[0], f_bias[0], mh_norm_g[0], w_out[0], norm2_g[0],
                  w_ffn_up[0], ffn_conv_w[0], ffn_conv_b[0], w_ffn_down[0], norm_f_g)
```

```python
import functools
import math

import jax
import jax.numpy as jnp
from jax import lax
from jax.experimental import pallas as pl
from jax.experimental.pallas import tpu as pltpu

F32 = jnp.float32
BF16 = jnp.bfloat16

D_MODEL = 1024
H_R, N_R = 8, 64
C_R = H_R * N_R
DECAY_LORA, AAA_LORA, GATE_LORA = 64, 64, 128
RW_COLS = 3 * C_R + DECAY_LORA + AAA_LORA + GATE_LORA
LNX_EPS = 64e-5
H_M, DK_M, DV_M = 4, 64, 128
C_MQK = H_M * DK_M
C_MV = H_M * DV_M
ML_COLS = 2 * C_MQK + 2 * C_MV + 2 * H_M
QK_CONV = 4
GATE_CAP = 15.0
D_FF = 2816
FFN_CONV = 3
NORM_EPS = 1e-6
CHUNK = 64

LANES = 128
SUBLANES = 8
BF16_ROWS = 16
ML_PAD = 1664
GATE_OFF = 2 * C_MQK + 2 * C_MV
VMEM_LIMIT = 56 * 1024 * 1024

EXP_M05 = math.exp(-0.5)


def _dot(a, b):
    return jnp.dot(a.astype(BF16), b.astype(BF16), preferred_element_type=F32)


def _split_hi_lo(x):
    hi = x.astype(BF16)
    lo = (x - hi.astype(F32)).astype(BF16)
    return hi, lo


def _sigmoid(x):
    return 1.0 / (1.0 + jnp.exp(-x))


def _rmsnorm_rows(x, g):
    ms = jnp.mean(x * x, axis=-1, keepdims=True)
    return x * lax.rsqrt(ms + NORM_EPS) * g


def _inproj_kernel(x_ref, g_ref, w_ref, mu_ref, cw_ref, cb_ref, orw_ref, oml_ref,
                   rcarry_ref, mcarry_ref, *, tm, tiles_per_seq):
    i = pl.program_id(0)

    @pl.when(i % tiles_per_seq == 0)
    def _():
        rcarry_ref[...] = jnp.zeros_like(rcarry_ref)
        mcarry_ref[...] = jnp.zeros_like(mcarry_ref)

    h = _rmsnorm_rows(x_ref[...], g_ref[...]).astype(BF16)
    row0 = lax.broadcasted_iota(jnp.int32, (tm, 1), 0) == 0
    step = 256
    for c in range(0, RW_COLS, step):
        cs = slice(c, c + step)
        p = jnp.dot(h, w_ref[:, cs], preferred_element_type=F32)
        shifted = jnp.where(row0, rcarry_ref[SUBLANES - 1:SUBLANES, cs], pltpu.roll(p, 1, axis=0))
        rcarry_ref[:, cs] = p[tm - SUBLANES:tm, :]
        orw_ref[:, cs] = p + (shifted - p) * mu_ref[:, cs]
    nqk = 2 * C_MQK
    for c in range(0, nqk, step):
        cs = slice(c, c + step)
        p = jnp.dot(h, w_ref[:, RW_COLS + c:RW_COLS + c + step], preferred_element_type=F32)
        ext = jnp.concatenate([mcarry_ref[:, cs], p], axis=0)
        mcarry_ref[:, cs] = p[tm - SUBLANES:tm, :]
        acc = cb_ref[:, cs] + p * cw_ref[QK_CONV - 1:QK_CONV, cs]
        for s in range(1, QK_CONV):
            sh = pltpu.roll(ext, s, axis=0)[SUBLANES:SUBLANES + tm]
            acc = acc + sh * cw_ref[QK_CONV - 1 - s:QK_CONV - s, cs]
        qk = acc * _sigmoid(acc)
        if c < C_MQK:
            qk = qk * (DK_M ** -0.5)
        oml_ref[:, cs] = qk
    for c in range(nqk, ML_PAD, step):
        n = min(step, ML_PAD - c)
        oml_ref[:, c:c + n] = jnp.dot(h, w_ref[:, RW_COLS + c:RW_COLS + c + n],
                                      preferred_element_type=F32)


def _inproj(x2d, g1, w_in_p, mu, cw, cb, tm, seq):
    n = x2d.shape[0]
    const = lambda shape: pl.BlockSpec(shape, lambda i: (0,) * len(shape))
    return pl.pallas_call(
        functools.partial(_inproj_kernel, tm=tm, tiles_per_seq=seq // tm),
        grid=(n // tm,),
        in_specs=[
            pl.BlockSpec((tm, D_MODEL), lambda i: (i, 0)),
            const((1, D_MODEL)),
            const((D_MODEL, RW_COLS + ML_PAD)),
            const((1, RW_COLS)),
            const((QK_CONV, 2 * C_MQK)),
            const((1, 2 * C_MQK)),
        ],
        out_specs=[
            pl.BlockSpec((tm, RW_COLS), lambda i: (i, 0)),
            pl.BlockSpec((tm, ML_PAD), lambda i: (i, 0)),
        ],
        out_shape=[
            jax.ShapeDtypeStruct((n, RW_COLS), F32),
            jax.ShapeDtypeStruct((n, ML_PAD), F32),
        ],
        scratch_shapes=[
            pltpu.VMEM((SUBLANES, RW_COLS), F32),
            pltpu.VMEM((SUBLANES, 2 * C_MQK), F32),
        ],
        compiler_params=pltpu.CompilerParams(
            dimension_semantics=("arbitrary",), vmem_limit_bytes=VMEM_LIMIT),
        name="inproj",
    )(x2d, g1, w_in_p, mu, cw, cb)


def _rwkv_stages(p_ref, vec_ref, wd_ref, wa_ref, wg_ref, o_ref, state_ref, tb):
    t_idx = pl.program_id(1)

    @pl.when(t_idx == 0)
    def _():
        state_ref[...] = jnp.zeros_like(state_ref)

    w0 = vec_ref[0:1, :]
    a0 = vec_ref[1:2, :]
    k_k = vec_ref[2:3, :]
    k_a = vec_ref[3:4, :]
    r_k = vec_ref[4:5, :]
    lnx_w = vec_ref[5:6, :]
    lnx_b = vec_ref[6:7, :]

    r = p_ref[:, 0:C_R]
    k = p_ref[:, C_R:2 * C_R]
    v = p_ref[:, 2 * C_R:3 * C_R]
    z = p_ref[:, 3 * C_R:3 * C_R + LANES]
    g_lo = p_ref[:, 3 * C_R + LANES:RW_COLS]

    u = w0 + _dot(jnp.tanh(z), wd_ref[...])
    ld = -EXP_M05 * _sigmoid(u)
    a = _sigmoid(a0 + _dot(z, wa_ref[...]))
    g = _dot(_sigmoid(g_lo), wg_ref[...])
    kk = k * k_k
    kmod = k * (1.0 + (a - 1.0) * k_a)
    rkr = r * kmod * r_k
    yield

    n2 = 2 * CHUNK
    ri = lax.broadcasted_iota(jnp.int32, (n2, n2), 0)
    ci = lax.broadcasted_iota(jnp.int32, (n2, n2), 1)
    strict_lower = ri > ci
    lower = ri >= ci
    eye = (ri == ci).astype(F32).astype(BF16)
    lvl_sel = []
    for l in range(6):
        m = (((ri >> (l + 1)) == (ci >> (l + 1))) & (((ri >> l) & 1) == 1) & (((ci >> l) & 1) == 0))
        lvl_sel.append(m.astype(F32).astype(BF16))
    lane_lo = lax.broadcasted_iota(jnp.int32, (CHUNK, LANES), 1) < N_R
    tri = (lax.broadcasted_iota(jnp.int32, (CHUNK, CHUNK), 0)
           >= lax.broadcasted_iota(jnp.int32, (CHUNK, CHUNK), 1)).astype(BF16)

    def stack2(x):
        return jnp.concatenate([jnp.where(lane_lo, x, 0.0), jnp.where(lane_lo, 0.0, x)], axis=0)

    def rep2(x):
        return jnp.concatenate([x, x], axis=0)

    def head_bcast(lo_col, hi_col):
        return jnp.where(lane_lo, lo_col, hi_col)

    def head_sums(x):
        return head_bcast(jnp.sum(jnp.where(lane_lo, x, 0.0), axis=1, keepdims=True),
                          jnp.sum(jnp.where(lane_lo, 0.0, x), axis=1, keepdims=True))

    n_chunks = tb // CHUNK
    n_pairs = H_R // 2
    zeros_sq = jnp.zeros((n2, n2), BF16)
    inv_n = 1.0 / N_R

    prob = []
    for c in range(n_chunks):
        sl = slice(c * CHUNK, (c + 1) * CHUNK)
        ld_c = ld[sl]
        hi, lo = _split_hi_lo(ld_c)
        cs = (jnp.dot(tri, hi, preferred_element_type=F32)
              + jnp.dot(tri, lo, preferred_element_type=F32))
        g_in = jnp.exp(cs)
        g_ex = jnp.exp(cs - ld_c)
        g_inv = jnp.exp(-cs)
        rt = r[sl] * g_in
        kt = kmod[sl] * g_inv
        ag = a[sl] * g_inv
        g_last = g_in[CHUNK - 1:CHUNK, :]
        for pr in range(n_pairs):
            ls = slice(pr * LANES, (pr + 1) * LANES)
            kk_st = stack2(kk[sl, ls])
            kkn_st = kk_st * jnp.minimum(lax.rsqrt(jnp.sum(kk_st * kk_st, axis=1, keepdims=True)), 1e12)
            kkn_bf = kkn_st.astype(BF16)
            p_a = kkn_bf * rep2((-g_ex[:, ls]).astype(BF16))
            p_b = kkn_bf * rep2(ag[:, ls].astype(BF16))
            p_r = stack2(rt[:, ls]).astype(BF16)
            p_k = stack2(kt[:, ls]).astype(BF16)
            v_f = stack2(v[sl, ls])
            v_st = v_f.astype(BF16)
            rkr_p = rkr[sl, ls]
            bonus = head_bcast(jnp.sum(jnp.where(lane_lo, rkr_p, 0.0), axis=1, keepdims=True),
                               jnp.sum(jnp.where(lane_lo, 0.0, rkr_p), axis=1, keepdims=True)) * v[sl, ls]
            p_bk = jnp.concatenate([p_b, p_k], axis=0)
            gram = lax.dot_general(jnp.concatenate([p_a, p_r], axis=0), p_bk,
                                   (((1,), (1,)), ((), ())), preferred_element_type=F32)
            g_ab = gram[0:n2, 0:n2].astype(BF16)
            gl_col = jnp.broadcast_to(g_last[:, ls], (n2, n2)).T
            prob.append(dict(
                p_a=p_a, p_r=p_r, p_bk=p_bk, v_st=v_st, g_ab=g_ab, bonus=bonus,
                a_ak=jnp.where(strict_lower, gram[0:n2, n2:2 * n2], 0.0).astype(BF16),
                a_rb=jnp.where(lower, gram[n2:2 * n2, 0:n2], 0.0).astype(BF16),
                a_rk=jnp.where(lower, gram[n2:2 * n2, n2:2 * n2], 0.0).astype(BF16),
                x=eye + g_ab * lvl_sel[0],
                g_last=gl_col))
        yield
    def odd_blocks(x, s):
        return jnp.concatenate([x[i * s:(i + 1) * s] for i in range(1, n2 // s, 2)], axis=0)

    def add_odd_blocks(x, u, s):
        parts = []
        for i in range(n2 // s):
            blk = x[i * s:(i + 1) * s]
            parts.append(blk + u[(i // 2) * s:(i // 2 + 1) * s] if i % 2 else blk)
        return jnp.concatenate(parts, axis=0)

    for l in range(1, 6):
        s = 1 << l
        sub = s >= BF16_ROWS
        for q in prob:
            lhs = odd_blocks(q["x"], s) if sub else q["x"]
            q["t"] = jnp.dot(lhs, q["g_ab"] * lvl_sel[l], preferred_element_type=F32).astype(BF16)
        yield
        for q in prob:
            u = jnp.dot(q["t"], q["x"], preferred_element_type=F32).astype(BF16)
            q["x"] = add_odd_blocks(q["x"], u, s) if sub else q["x"] + u
        yield
    for q in prob:
        s1 = jnp.dot(jnp.concatenate([q["a_ak"], q["a_rk"]], axis=0), q["v_st"],
                     preferred_element_type=F32)
        q["wv"] = s1[0:n2].astype(BF16)
        q["arkv"] = s1[n2:2 * n2]
    yield
    for q in prob:
        q["xc"] = jnp.dot(q["x"], jnp.concatenate([q["wv"], q["p_a"]], axis=1),
                          preferred_element_type=F32).astype(BF16)
    yield
    for q in prob:
        s2 = jnp.dot(q["a_rb"], q["xc"], preferred_element_type=F32)
        q["yv"] = s2[:, 0:n2] + q["arkv"]
        r_eff = q["p_r"].astype(F32) + s2[:, n2:2 * n2]
        rhs = jnp.concatenate([q["xc"], jnp.concatenate([q["v_st"], zeros_sq], axis=1)], axis=0)
        hb = lax.dot_general(q["p_bk"], rhs, (((0,), (0,)), ((), ())),
                             preferred_element_type=F32)
        q["h"] = hb[:, 0:n2]
        q["lhs"] = jnp.concatenate([r_eff.astype(BF16), hb[:, n2:2 * n2].astype(BF16)], axis=0)
    yield

    states = [state_ref[i] for i in range(n_pairs)]
    for c in range(n_chunks):
        for pr in range(n_pairs):
            q = prob[c * n_pairs + pr]
            m0 = states[pr]
            o2 = jnp.dot(q["lhs"], m0.astype(BF16), preferred_element_type=F32)
            states[pr] = q["g_last"] * (m0 + o2[n2:2 * n2] + q["h"])
            y_st = o2[0:n2] + q["yv"]
            q["y"] = y_st[0:CHUNK] + y_st[CHUNK:n2]
        yield
    for pr in range(n_pairs):
        state_ref[pr] = states[pr]
    for q in prob:
        q["mean"] = head_sums(q["y"]) * inv_n
    yield
    for q in prob:
        q["d"] = q["y"] - q["mean"]
        q["var"] = head_sums(q["d"] * q["d"]) * inv_n
    yield
    for c in range(n_chunks):
        sl = slice(c * CHUNK, (c + 1) * CHUNK)
        for pr in range(n_pairs):
            ls = slice(pr * LANES, (pr + 1) * LANES)
            q = prob[c * n_pairs + pr]
            y = q["d"] * lax.rsqrt(q["var"] + LNX_EPS) * lnx_w[:, ls] + lnx_b[:, ls] + q["bonus"]
            o_ref[sl, ls] = (y * g[sl, ls]).astype(o_ref.dtype)
    yield


def _mlstm_stages(p_ref, gb_ref, ng_ref, o_ref, cn_ref, m_ref, tb):
    t_idx = pl.program_id(1)

    @pl.when(t_idx == 0)
    def _():
        cn_ref[...] = jnp.zeros_like(cn_ref)
        m_ref[...] = jnp.zeros_like(m_ref)

    nqk = 2 * C_MQK
    q_all = p_ref[:, 0:C_MQK]
    k_all = p_ref[:, C_MQK:nqk]
    v_all = p_ref[:, nqk:nqk + C_MV]
    o_gate = p_ref[:, nqk + C_MV:nqk + 2 * C_MV]

    graw = p_ref[:, GATE_OFF:GATE_OFF + LANES] + gb_ref[...]
    capped = GATE_CAP * jnp.tanh(graw / GATE_CAP)
    logsig = jnp.minimum(capped, 0.0) - jnp.log(1.0 + jnp.exp(-jnp.abs(capped)))
    lane_g = lax.broadcasted_iota(jnp.int32, (tb, LANES), 1)
    gates = jnp.where(lane_g < H_M, capped, logsig)

    ig_sh = pltpu.roll(gates, H_M, axis=1)
    yield

    n2 = 2 * CHUNK
    tri = (lax.broadcasted_iota(jnp.int32, (CHUNK, CHUNK), 0)
           >= lax.broadcasted_iota(jnp.int32, (CHUNK, CHUNK), 1)).astype(BF16)
    lane_lo = lax.broadcasted_iota(jnp.int32, (CHUNK, LANES), 1) < DK_M
    causal_st = ((lax.broadcasted_iota(jnp.int32, (n2, CHUNK), 0) & (CHUNK - 1))
                 >= lax.broadcasted_iota(jnp.int32, (n2, CHUNK), 1))
    zpad = jnp.zeros((CHUNK, LANES), F32)
    ones_v = jnp.ones((CHUNK, DV_M), BF16)
    ones_sq = jnp.ones((DV_M, DV_M), BF16)
    row_sq = lax.broadcasted_iota(jnp.int32, (n2, LANES), 0)
    lane_sq = lax.broadcasted_iota(jnp.int32, (n2, LANES), 1)
    row_lo = row_sq < CHUNK
    n_chunks = tb // CHUNK
    n_pairs = H_M // 2
    gate_lane = [lane_sq == jnp.where(row_lo, H_M + 2 * pr, H_M + 2 * pr + 1) for pr in range(n_pairs)]
    spread = [(row_sq == jnp.where(lane_sq < DK_M, H_M + 2 * pr, H_M + 2 * pr + 1)).astype(BF16)
              for pr in range(n_pairs)]

    def stack2(x):
        return jnp.concatenate([jnp.where(lane_lo, x, 0.0), jnp.where(lane_lo, 0.0, x)], axis=0)

    def rep2(x):
        return jnp.concatenate([x, x], axis=0)

    def rows2(x0, x1, shape):
        return jnp.concatenate([jnp.broadcast_to(x0, shape), jnp.broadcast_to(x1, shape)], axis=0)

    prob = []
    tn = (((0,), (0,)), ((), ()))
    for c in range(n_chunks):
        sl = slice(c * CHUNK, (c + 1) * CHUNK)
        hi, lo = _split_hi_lo(gates[sl])
        cs = (jnp.dot(tri, hi, preferred_element_type=F32)
              + jnp.dot(tri, lo, preferred_element_type=F32))
        c_all = ig_sh[sl] - cs
        w_all = jnp.exp(c_all - jnp.max(c_all, axis=0, keepdims=True)).astype(BF16)
        c_t = jnp.concatenate([c_all, zpad], axis=0).T
        cs2 = rep2(cs)
        for pr in range(n_pairs):
            prob.append(dict(c=c, pr=pr, sl=sl, w_all=w_all, c_t=c_t, cs2=cs2))
    yield
    for q in prob:
        ls = slice(q["pr"] * LANES, (q["pr"] + 1) * LANES)
        q["q_st"] = stack2(q_all[q["sl"], ls]).astype(BF16)
        q["k_sl"] = k_all[q["sl"], ls]
        q["s_qk"] = lax.dot_general(q["q_st"], q["k_sl"].astype(BF16), (((1,), (1,)), ((), ())),
                                    preferred_element_type=F32)
    yield
    for q in prob:
        h0, h1 = 2 * q["pr"], 2 * q["pr"] + 1
        q["b_col"] = jnp.sum(jnp.where(gate_lane[q["pr"]], q["cs2"], 0.0), axis=1, keepdims=True)
        c_row = rows2(q["c_t"][H_M + h0:H_M + h0 + 1, 0:CHUNK], q["c_t"][H_M + h1:H_M + h1 + 1, 0:CHUNK],
                      (CHUNK, CHUNK))
        q["masked"] = jnp.where(causal_st, c_row, -jnp.inf)
        q["cm"] = jnp.max(q["masked"], axis=1, keepdims=True)
    yield
    for q in prob:
        h0, h1 = 2 * q["pr"], 2 * q["pr"] + 1
        q["s_d"] = (q["s_qk"] * jnp.exp(q["masked"] - q["cm"])).astype(BF16)
        q["v0"] = jnp.concatenate([v_all[q["sl"], h0 * DV_M:(h0 + 1) * DV_M].astype(BF16), ones_v], axis=1)
        q["v1"] = jnp.concatenate([v_all[q["sl"], h1 * DV_M:(h1 + 1) * DV_M].astype(BF16), ones_v], axis=1)
        q["kw"] = stack2(q["k_sl"] * jnp.dot(q["w_all"], spread[q["pr"]],
                                             preferred_element_type=F32)).astype(BF16)
    yield
    for q in prob:
        q["sv"] = jnp.concatenate([jnp.dot(q["s_d"][0:CHUNK], q["v0"], preferred_element_type=F32),
                                   jnp.dot(q["s_d"][CHUNK:n2], q["v1"], preferred_element_type=F32)], axis=0)
        q["kv"] = lax.dot_general(q["kw"], jnp.concatenate([q["v0"], q["v1"]], axis=0), tn,
                                  preferred_element_type=F32)
        q["b_last"] = rows2(q["b_col"][CHUNK - 1:CHUNK], q["b_col"][n2 - 1:n2], (CHUNK, 1))
        q["a_max"] = q["b_last"] + rows2(q["cm"][CHUNK - 1:CHUNK], q["cm"][n2 - 1:n2], (CHUNK, 1))
    yield

    cn_st = [cn_ref[i] for i in range(n_pairs)]
    m_st = [m_ref[i] for i in range(n_pairs)]
    for q in prob:
        pr = q["pr"]
        q["m_prev"] = m_st[pr]
        q["cn_prev"] = cn_st[pr].astype(BF16)
        m_new = jnp.maximum(q["b_last"] + q["m_prev"], q["a_max"])
        dec = jnp.exp(q["b_last"] + q["m_prev"] - m_new)
        beta = jnp.exp(q["a_max"] - m_new)
        cn_st[pr] = dec * cn_st[pr] + beta * q["kv"]
        m_st[pr] = m_new
    for pr in range(n_pairs):
        cn_ref[pr] = cn_st[pr]
        m_ref[pr] = m_st[pr]
    yield
    for q in prob:
        q["mx"] = jnp.maximum(q["cm"], q["m_prev"])
        q["qc"] = jnp.dot(q["q_st"], q["cn_prev"], preferred_element_type=F32)
    yield
    for q in prob:
        mix = jnp.exp(q["cm"] - q["mx"]) * q["sv"] + jnp.exp(q["m_prev"] - q["mx"]) * q["qc"]
        num = mix[:, 0:DV_M]
        den = mix[:, DV_M:2 * DV_M]
        q["hout"] = num / jnp.maximum(jnp.abs(den), jnp.exp(-(q["b_col"] + q["mx"])))
    yield
    for q in prob:
        q["ms"] = jnp.dot((q["hout"] * q["hout"]).astype(BF16), ones_sq,
                          preferred_element_type=F32) * (1.0 / DV_M)
    yield
    y_rows = []
    for c in range(n_chunks):
        y_heads = []
        for pr in range(n_pairs):
            q = prob[c * n_pairs + pr]
            hn = q["hout"] * lax.rsqrt(q["ms"] + NORM_EPS)
            y_heads.append(hn[0:CHUNK])
            y_heads.append(hn[CHUNK:n2])
        y_rows.append(jnp.concatenate(y_heads, axis=1))
    y = jnp.concatenate(y_rows, axis=0)
    o_ref[...] = (y * ng_ref[...] * _sigmoid(o_gate)).astype(o_ref.dtype)
    yield


RWKV_LEAD_STAGES = 9
RWKV_STAGES_PER_MLSTM = 2


def _timemix_kernel(prw_ref, vec_ref, wd_ref, wa_ref, wg_ref, pml_ref, gb_ref, ng_ref,
                    orw_ref, oml_ref, rstate_ref, cn_ref, m_ref, *, tb):
    rw = _rwkv_stages(prw_ref, vec_ref, wd_ref, wa_ref, wg_ref, orw_ref, rstate_ref, tb)
    ml = _mlstm_stages(pml_ref, gb_ref, ng_ref, oml_ref, cn_ref, m_ref, tb)
    for _ in range(RWKV_LEAD_STAGES):
        next(rw)
    order = [rw] * RWKV_STAGES_PER_MLSTM + [ml]
    done = set()
    while len(done) < 2:
        for gen in order:
            if id(gen) not in done and next(gen, StopIteration) is StopIteration:
                done.add(id(gen))


def _timemix(p_rw, vecs, wd, wa, wg, p_ml, gb, ng, tb):
    bsz, t, _ = p_rw.shape
    const = lambda shape: pl.BlockSpec(shape, lambda b, i: (0,) * len(shape))
    blk = lambda width: pl.BlockSpec((None, tb, width), lambda b, i: (b, i, 0))
    return pl.pallas_call(
        functools.partial(_timemix_kernel, tb=tb),
        grid=(bsz, t // tb),
        in_specs=[
            blk(RW_COLS),
            const((SUBLANES, C_R)),
            const((LANES, C_R)),
            const((LANES, C_R)),
            const((GATE_LORA, C_R)),
            blk(ML_PAD),
            const((1, LANES)),
            const((1, C_MV)),
        ],
        out_specs=[blk(C_R), blk(C_MV)],
        out_shape=[jax.ShapeDtypeStruct((bsz, t, C_R), BF16),
                   jax.ShapeDtypeStruct((bsz, t, C_MV), BF16)],
        scratch_shapes=[
            pltpu.VMEM((H_R // 2, 2 * N_R, 2 * N_R), F32),
            pltpu.VMEM((H_M // 2, 2 * DK_M, 2 * DV_M), F32),
            pltpu.VMEM((H_M // 2, 2 * CHUNK, 1), F32),
        ],
        compiler_params=pltpu.CompilerParams(
            dimension_semantics=("parallel", "arbitrary"), vmem_limit_bytes=VMEM_LIMIT),
        name="timemix",
    )(p_rw, vecs, wd, wa, wg, p_ml, gb, ng)


def _ffn_kernel(x_ref, yr_ref, ym_ref, wo_ref, g2_ref, wup_ref, cw_ref, cb_ref, wdn_ref, gf_ref,
                o_ref, x1_ref, h2_ref, act_ref, carry_ref, *, tm, tiles_per_seq):
    i = pl.program_id(0)

    @pl.when(i % tiles_per_seq == 0)
    def _():
        carry_ref[...] = jnp.zeros_like(carry_ref)

    x1 = (x_ref[...]
          + jnp.dot(yr_ref[...], wo_ref[0:C_R, :], preferred_element_type=F32)
          + jnp.dot(ym_ref[...], wo_ref[C_R:C_R + C_MV, :], preferred_element_type=F32))
    x1_ref[...] = x1
    h2_ref[...] = _rmsnorm_rows(x1, g2_ref[...]).astype(BF16)

    fc = 256
    for j in range(D_FF // fc):
        cs = slice(j * fc, (j + 1) * fc)
        h2 = h2_ref[...]
        a = jnp.dot(h2, wup_ref[:, j * fc:(j + 1) * fc], preferred_element_type=F32)
        b = jnp.dot(h2, wup_ref[:, D_FF + j * fc:D_FF + (j + 1) * fc], preferred_element_type=F32)
        ext = jnp.concatenate([carry_ref[:, cs], a], axis=0)
        carry_ref[:, cs] = a[tm - SUBLANES:tm, :]
        conv = cb_ref[:, cs] + a * cw_ref[FFN_CONV - 1:FFN_CONV, cs]
        for s in range(1, FFN_CONV):
            sh = pltpu.roll(ext, s, axis=0)[SUBLANES:SUBLANES + tm]
            conv = conv + sh * cw_ref[FFN_CONV - 1 - s:FFN_CONV - s, cs]
        act_ref[:, cs] = (conv * _sigmoid(conv) * b).astype(BF16)

    nc = 256
    for j in range(D_MODEL // nc):
        cs = slice(j * nc, (j + 1) * nc)
        x1_ref[:, cs] = x1_ref[:, cs] + jnp.dot(act_ref[...], wdn_ref[:, cs],
                                                preferred_element_type=F32)
    o_ref[...] = _rmsnorm_rows(x1_ref[...], gf_ref[...])


def _ffn(x2d, y_rw, y_ml, w_out, g2, w_up, cw, cb, w_dn, gf, tm, seq):
    n = x2d.shape[0]
    const = lambda shape: pl.BlockSpec(shape, lambda i: (0,) * len(shape),
                                       pipeline_mode=pl.Buffered(1))
    return pl.pallas_call(
        functools.partial(_ffn_kernel, tm=tm, tiles_per_seq=seq // tm),
        grid=(n // tm,),
        in_specs=[
            pl.BlockSpec((tm, D_MODEL), lambda i: (i, 0)),
            pl.BlockSpec((tm, C_R), lambda i: (i, 0)),
            pl.BlockSpec((tm, C_MV), lambda i: (i, 0)),
            const((C_R + C_MV, D_MODEL)),
            const((1, D_MODEL)),
            const((D_MODEL, 2 * D_FF)),
            const((FFN_CONV, D_FF)),
            const((1, D_FF)),
            const((D_FF, D_MODEL)),
            const((1, D_MODEL)),
        ],
        out_specs=pl.BlockSpec((tm, D_MODEL), lambda i: (i, 0)),
        out_shape=jax.ShapeDtypeStruct((n, D_MODEL), F32),
        scratch_shapes=[
            pltpu.VMEM((tm, D_MODEL), F32),
            pltpu.VMEM((tm, D_MODEL), BF16),
            pltpu.VMEM((tm, D_FF), BF16),
            pltpu.VMEM((SUBLANES, D_FF), F32),
        ],
        compiler_params=pltpu.CompilerParams(
            dimension_semantics=("arbitrary",), vmem_limit_bytes=VMEM_LIMIT),
        name="outproj_convffn",
    )(x2d, y_rw, y_ml, w_out, g2, w_up, cw, cb, w_dn, gf)


def _row(v):
    return v.reshape(1, -1)


def _layer(x, norm1_g, w_in, rw_mu, w0, w_up_decay, a0, w_up_a, w_up_g, k_k, k_a, r_k,
           lnx_w, lnx_b, qk_conv_w, qk_conv_b, i_bias, f_bias, mh_norm_g, w_out,
           norm2_g, w_ffn_up, ffn_conv_w, ffn_conv_b, w_ffn_down, out_g):
    bsz, seq, _ = x.shape
    n = bsz * seq
    tm = min(1024, seq)
    tb = min(256, seq)
    x2d = x.reshape(n, D_MODEL)

    w_in_p = jnp.pad(w_in, ((0, 0), (0, ML_PAD - ML_COLS))).astype(BF16)
    p_rw, p_ml = _inproj(x2d, _row(norm1_g), w_in_p, _row(rw_mu), qk_conv_w, _row(qk_conv_b), tm, seq)

    zeros_lora = jnp.zeros((DECAY_LORA, C_R), F32)
    wd = jnp.concatenate([w_up_decay, zeros_lora], axis=0).astype(BF16)
    wa = jnp.concatenate([zeros_lora, w_up_a], axis=0).astype(BF16)
    vecs = jnp.stack([w0, a0, k_k, k_a, r_k.reshape(C_R), lnx_w, lnx_b, jnp.zeros_like(w0)], axis=0)
    gate_bias = jnp.pad(jnp.concatenate([i_bias, f_bias]), (0, LANES - 2 * H_M))
    y_rw, y_ml = _timemix(p_rw.reshape(bsz, seq, RW_COLS), vecs, wd, wa, w_up_g.astype(BF16),
                          p_ml.reshape(bsz, seq, ML_PAD), _row(gate_bias), _row(mh_norm_g), tb)

    out = _ffn(x2d, y_rw.reshape(n, C_R), y_ml.reshape(n, C_MV), w_out.astype(BF16), _row(norm2_g),
               w_ffn_up.astype(BF16), ffn_conv_w, _row(ffn_conv_b), w_ffn_down.astype(BF16),
               _row(out_g), tm, seq)
    return out.reshape(bsz, seq, D_MODEL)


def kernel(x, norm1_g, w_in, rw_mu, w0, w_up_decay, a0, w_up_a, w_up_g, k_k, k_a, r_k, lnx_w, lnx_b,
           qk_conv_w, qk_conv_b, i_bias, f_bias, mh_norm_g, w_out, norm2_g, w_ffn_up, ffn_conv_w,
           ffn_conv_b, w_ffn_down, norm_f_g):
    assert norm1_g.shape[0] == 1
    return _layer(x, norm1_g[0], w_in[0], rw_mu[0], w0[0], w_up_decay[0], a0[0], w_up_a[0],
                  w_up_g[0], k_k[0], k_a[0], r_k[0], lnx_w[0], lnx_b[0], qk_conv_w[0],
                  qk_conv_b[0], i_bias[0], f_bias[0], mh_norm_g[0], w_out[0], norm2_g[0],
                  w_ffn_up[0], ffn_conv_w[0], ffn_conv_b[0], w_ffn_down[0], norm_f_g)
```

```python
import functools
import math

import jax
import jax.numpy as jnp
from jax import lax
from jax.experimental import pallas as pl
from jax.experimental.pallas import tpu as pltpu

F32 = jnp.float32
BF16 = jnp.bfloat16

D_MODEL = 1024
H_R, N_R = 8, 64
C_R = H_R * N_R
DECAY_LORA, AAA_LORA, GATE_LORA = 64, 64, 128
RW_COLS = 3 * C_R + DECAY_LORA + AAA_LORA + GATE_LORA
LNX_EPS = 64e-5
H_M, DK_M, DV_M = 4, 64, 128
C_MQK = H_M * DK_M
C_MV = H_M * DV_M
ML_COLS = 2 * C_MQK + 2 * C_MV + 2 * H_M
QK_CONV = 4
GATE_CAP = 15.0
D_FF = 2816
FFN_CONV = 3
NORM_EPS = 1e-6
CHUNK = 64

LANES = 128
SUBLANES = 8
BF16_ROWS = 16
ML_PAD = 1664
GATE_OFF = 2 * C_MQK + 2 * C_MV
VMEM_LIMIT = 56 * 1024 * 1024

EXP_M05 = math.exp(-0.5)


def _dot(a, b):
    return jnp.dot(a.astype(BF16), b.astype(BF16), preferred_element_type=F32)


def _split_hi_lo(x):
    hi = x.astype(BF16)
    lo = (x - hi.astype(F32)).astype(BF16)
    return hi, lo


def _sigmoid(x):
    return 1.0 / (1.0 + jnp.exp(-x))


def _rmsnorm_rows(x, g):
    ms = jnp.mean(x * x, axis=-1, keepdims=True)
    return x * lax.rsqrt(ms + NORM_EPS) * g


def _inproj_kernel(x_ref, g_ref, w_ref, mu_ref, cw_ref, cb_ref, orw_ref, oml_ref,
                   rcarry_ref, mcarry_ref, *, tm, tiles_per_seq):
    i = pl.program_id(0)

    @pl.when(i % tiles_per_seq == 0)
    def _():
        rcarry_ref[...] = jnp.zeros_like(rcarry_ref)
        mcarry_ref[...] = jnp.zeros_like(mcarry_ref)

    h = _rmsnorm_rows(x_ref[...], g_ref[...]).astype(BF16)
    row0 = lax.broadcasted_iota(jnp.int32, (tm, 1), 0) == 0
    step = 256
    for c in range(0, RW_COLS, step):
        cs = slice(c, c + step)
        p = jnp.dot(h, w_ref[:, cs], preferred_element_type=F32)
        shifted = jnp.where(row0, rcarry_ref[SUBLANES - 1:SUBLANES, cs], pltpu.roll(p, 1, axis=0))
        rcarry_ref[:, cs] = p[tm - SUBLANES:tm, :]
        orw_ref[:, cs] = p + (shifted - p) * mu_ref[:, cs]
    nqk = 2 * C_MQK
    for c in range(0, nqk, step):
        cs = slice(c, c + step)
        p = jnp.dot(h, w_ref[:, RW_COLS + c:RW_COLS + c + step], preferred_element_type=F32)
        ext = jnp.concatenate([mcarry_ref[:, cs], p], axis=0)
        mcarry_ref[:, cs] = p[tm - SUBLANES:tm, :]
        acc = cb_ref[:, cs] + p * cw_ref[QK_CONV - 1:QK_CONV, cs]
        for s in range(1, QK_CONV):
            sh = pltpu.roll(ext, s, axis=0)[SUBLANES:SUBLANES + tm]
            acc = acc + sh * cw_ref[QK_CONV - 1 - s:QK_CONV - s, cs]
        qk = acc * _sigmoid(acc)
        if c < C_MQK:
            qk = qk * (DK_M ** -0.5)
        oml_ref[:, cs] = qk
    for c in range(nqk, ML_PAD, step):
        n = min(step, ML_PAD - c)
        oml_ref[:, c:c + n] = jnp.dot(h, w_ref[:, RW_COLS + c:RW_COLS + c + n],
                                      preferred_element_type=F32)


def _inproj(x2d, g1, w_in_p, mu, cw, cb, tm, seq):
    n = x2d.shape[0]
    const = lambda shape: pl.BlockSpec(shape, lambda i: (0,) * len(shape))
    return pl.pallas_call(
        functools.partial(_inproj_kernel, tm=tm, tiles_per_seq=seq // tm),
        grid=(n // tm,),
        in_specs=[
            pl.BlockSpec((tm, D_MODEL), lambda i: (i, 0)),
            const((1, D_MODEL)),
            const((D_MODEL, RW_COLS + ML_PAD)),
            const((1, RW_COLS)),
            const((QK_CONV, 2 * C_MQK)),
            const((1, 2 * C_MQK)),
        ],
        out_specs=[
            pl.BlockSpec((tm, RW_COLS), lambda i: (i, 0)),
            pl.BlockSpec((tm, ML_PAD), lambda i: (i, 0)),
        ],
        out_shape=[
            jax.ShapeDtypeStruct((n, RW_COLS), F32),
            jax.ShapeDtypeStruct((n, ML_PAD), F32),
        ],
        scratch_shapes=[
            pltpu.VMEM((SUBLANES, RW_COLS), F32),
            pltpu.VMEM((SUBLANES, 2 * C_MQK), F32),
        ],
        compiler_params=pltpu.CompilerParams(
            dimension_semantics=("arbitrary",), vmem_limit_bytes=VMEM_LIMIT),
        name="inproj",
    )(x2d, g1, w_in_p, mu, cw, cb)


def _rwkv_stages(p_ref, vec_ref, wd_ref, wa_ref, wg_ref, o_ref, state_ref, tb):
    t_idx = pl.program_id(1)

    @pl.when(t_idx == 0)
    def _():
        state_ref[...] = jnp.zeros_like(state_ref)

    w0 = vec_ref[0:1, :]
    a0 = vec_ref[1:2, :]
    k_k = vec_ref[2:3, :]
    k_a = vec_ref[3:4, :]
    r_k = vec_ref[4:5, :]
    lnx_w = vec_ref[5:6, :]
    lnx_b = vec_ref[6:7, :]

    r = p_ref[:, 0:C_R]
    k = p_ref[:, C_R:2 * C_R]
    v = p_ref[:, 2 * C_R:3 * C_R]
    z = p_ref[:, 3 * C_R:3 * C_R + LANES]
    g_lo = p_ref[:, 3 * C_R + LANES:RW_COLS]

    u = w0 + _dot(jnp.tanh(z), wd_ref[...])
    ld = -EXP_M05 * _sigmoid(u)
    a = _sigmoid(a0 + _dot(z, wa_ref[...]))
    g = _dot(_sigmoid(g_lo), wg_ref[...])
    kk = k * k_k
    kmod = k * (1.0 + (a - 1.0) * k_a)
    rkr = r * kmod * r_k
    yield

    n2 = 2 * CHUNK
    ri = lax.broadcasted_iota(jnp.int32, (n2, n2), 0)
    ci = lax.broadcasted_iota(jnp.int32, (n2, n2), 1)
    same_head = (ri < CHUNK) == (ci < N_R)
    bd_sel = same_head.astype(F32).astype(BF16)
    ti = lax.broadcasted_iota(jnp.int32, (CHUNK, LANES), 0)
    si = lax.broadcasted_iota(jnp.int32, (CHUNK, LANES), 1) & (CHUNK - 1)
    strict_lower = ti > si
    lower = ti >= si
    eye = (ti == si).astype(F32).astype(BF16)
    lvl_fl, lvl_bd = [], []
    for l in range(6):
        m = (((ti >> (l + 1)) == (si >> (l + 1))) & (((ti >> l) & 1) == 1) & (((si >> l) & 1) == 0))
        lvl_fl.append(m.astype(F32).astype(BF16))
        rs_, cs_ = ri & (CHUNK - 1), ci & (CHUNK - 1)
        m = (same_head & ((rs_ >> (l + 1)) == (cs_ >> (l + 1)))
             & (((rs_ >> l) & 1) == 1) & (((cs_ >> l) & 1) == 0))
        lvl_bd.append(m.astype(F32).astype(BF16))
    lane_lo = lax.broadcasted_iota(jnp.int32, (CHUNK, LANES), 1) < N_R
    tri = (lax.broadcasted_iota(jnp.int32, (CHUNK, CHUNK), 0)
           >= lax.broadcasted_iota(jnp.int32, (CHUNK, CHUNK), 1)).astype(BF16)

    def bd(x):
        return jnp.concatenate([x, x], axis=0) * bd_sel

    def head_bcast(lo_col, hi_col):
        return jnp.where(lane_lo, lo_col, hi_col)

    def head_sums(x):
        return head_bcast(jnp.sum(jnp.where(lane_lo, x, 0.0), axis=1, keepdims=True),
                          jnp.sum(jnp.where(lane_lo, 0.0, x), axis=1, keepdims=True))

    n_chunks = tb // CHUNK
    n_pairs = H_R // 2
    zeros_fl = jnp.zeros((CHUNK, LANES), BF16)
    inv_n = 1.0 / N_R

    prob = []
    for c in range(n_chunks):
        sl = slice(c * CHUNK, (c + 1) * CHUNK)
        ld_c = ld[sl]
        hi, lo = _split_hi_lo(ld_c)
        cs = (jnp.dot(tri, hi, preferred_element_type=F32)
              + jnp.dot(tri, lo, preferred_element_type=F32))
        g_in = jnp.exp(cs)
        g_ex = jnp.exp(cs - ld_c)
        g_inv = jnp.exp(-cs)
        rt = r[sl] * g_in
        kt = kmod[sl] * g_inv
        ag = a[sl] * g_inv
        g_last = g_in[CHUNK - 1:CHUNK, :]
        for pr in range(n_pairs):
            ls = slice(pr * LANES, (pr + 1) * LANES)
            kk_p = kk[sl, ls]
            kkn = kk_p * jnp.minimum(lax.rsqrt(head_sums(kk_p * kk_p)), 1e12)
            a_fl = (-kkn * g_ex[:, ls]).astype(BF16)
            b_fl = (kkn * ag[:, ls]).astype(BF16)
            r_fl = rt[:, ls].astype(BF16)
            k_fl = kt[:, ls].astype(BF16)
            v_fl = v[sl, ls].astype(BF16)
            rkr_p = rkr[sl, ls]
            bonus = head_bcast(jnp.sum(jnp.where(lane_lo, rkr_p, 0.0), axis=1, keepdims=True),
                               jnp.sum(jnp.where(lane_lo, 0.0, rkr_p), axis=1, keepdims=True)) * v[sl, ls]
            gram = lax.dot_general(jnp.concatenate([a_fl, r_fl], axis=0),
                                   jnp.concatenate([bd(b_fl), bd(k_fl)], axis=0),
                                   (((1,), (1,)), ((), ())), preferred_element_type=F32)
            g_ab = gram[0:CHUNK, 0:n2].astype(BF16)
            x_fl = eye + g_ab * lvl_fl[0]
            gl_col = jnp.broadcast_to(g_last[:, ls], (n2, n2)).T
            prob.append(dict(
                a_bd=bd(a_fl), r_fl=r_fl, bk_fl=jnp.concatenate([b_fl, k_fl], axis=0), v_fl=v_fl, v_bd=bd(v_fl),
                g_bd=bd(g_ab), bonus=bonus,
                a_ak=jnp.where(strict_lower, gram[0:CHUNK, n2:2 * n2], 0.0).astype(BF16),
                a_rb=jnp.where(lower, gram[CHUNK:n2, 0:n2], 0.0).astype(BF16),
                a_rk=jnp.where(lower, gram[CHUNK:n2, n2:2 * n2], 0.0).astype(BF16),
                x=x_fl, x_bd=bd(x_fl), g_last=gl_col))
        yield
    def odd_blocks(x, s):
        return jnp.concatenate([x[i * s:(i + 1) * s] for i in range(1, CHUNK // s, 2)], axis=0)

    def add_odd_blocks(x, u, s):
        parts = []
        for i in range(CHUNK // s):
            blk = x[i * s:(i + 1) * s]
            parts.append(blk + u[(i // 2) * s:(i // 2 + 1) * s] if i % 2 else blk)
        return jnp.concatenate(parts, axis=0)

    for l in range(1, 6):
        s = 1 << l
        sub = s >= BF16_ROWS
        for q in prob:
            lhs = odd_blocks(q["x"], s) if sub else q["x"]
            q["t"] = jnp.dot(lhs, q["g_bd"] * lvl_bd[l], preferred_element_type=F32).astype(BF16)
        yield
        for q in prob:
            u = jnp.dot(q["t"], q["x_bd"], preferred_element_type=F32).astype(BF16)
            q["x"] = add_odd_blocks(q["x"], u, s) if sub else q["x"] + u
            q["x_bd"] = bd(q["x"])
        yield
    for q in prob:
        s1 = jnp.dot(jnp.concatenate([q["a_ak"], q["a_rk"]], axis=0), q["v_bd"],
                     preferred_element_type=F32)
        q["wv"] = s1[0:CHUNK].astype(BF16)
        q["arkv"] = s1[CHUNK:n2]
    yield
    for q in prob:
        q["xc"] = jnp.dot(q["x"], jnp.concatenate([bd(q["wv"]), q["a_bd"]], axis=1),
                          preferred_element_type=F32).astype(BF16)
    yield
    for q in prob:
        xwv, xpa = q["xc"][:, 0:n2], q["xc"][:, n2:2 * n2]
        s2 = jnp.dot(q["a_rb"], jnp.concatenate([bd(xwv), bd(xpa)], axis=1),
                     preferred_element_type=F32)
        q["yv"] = s2[:, 0:n2] + q["arkv"]
        r_eff = q["r_fl"].astype(F32) + s2[:, n2:2 * n2]
        rhs = jnp.concatenate([q["xc"], jnp.concatenate([q["v_fl"], zeros_fl], axis=1)], axis=0)
        hb = lax.dot_general(q["bk_fl"], rhs, (((0,), (0,)), ((), ())),
                             preferred_element_type=F32)
        q["h"] = jnp.where(same_head, hb[:, 0:n2], 0.0)
        q["lhs"] = jnp.concatenate([r_eff.astype(BF16),
                                    jnp.where(same_head, hb[:, n2:2 * n2], 0.0).astype(BF16)], axis=0)
    yield

    states = [state_ref[i] for i in range(n_pairs)]
    for c in range(n_chunks):
        for pr in range(n_pairs):
            q = prob[c * n_pairs + pr]
            m0 = states[pr]
            o2 = jnp.dot(q["lhs"], m0.astype(BF16), preferred_element_type=F32)
            states[pr] = q["g_last"] * (m0 + o2[CHUNK:CHUNK + n2] + q["h"])
            q["y"] = o2[0:CHUNK] + q["yv"]
        yield
    for pr in range(n_pairs):
        state_ref[pr] = states[pr]
    for q in prob:
        q["mean"] = head_sums(q["y"]) * inv_n
    yield
    for q in prob:
        q["d"] = q["y"] - q["mean"]
        q["var"] = head_sums(q["d"] * q["d"]) * inv_n
    yield
    for c in range(n_chunks):
        sl = slice(c * CHUNK, (c + 1) * CHUNK)
        for pr in range(n_pairs):
            ls = slice(pr * LANES, (pr + 1) * LANES)
            q = prob[c * n_pairs + pr]
            y = q["d"] * lax.rsqrt(q["var"] + LNX_EPS) * lnx_w[:, ls] + lnx_b[:, ls] + q["bonus"]
            o_ref[sl, ls] = (y * g[sl, ls]).astype(o_ref.dtype)
    yield


def _mlstm_stages(p_ref, gb_ref, ng_ref, o_ref, cn_ref, m_ref, tb):
    t_idx = pl.program_id(1)

    @pl.when(t_idx == 0)
    def _():
        cn_ref[...] = jnp.zeros_like(cn_ref)
        m_ref[...] = jnp.zeros_like(m_ref)

    nqk = 2 * C_MQK
    q_all = p_ref[:, 0:C_MQK]
    k_all = p_ref[:, C_MQK:nqk]
    v_all = p_ref[:, nqk:nqk + C_MV]
    o_gate = p_ref[:, nqk + C_MV:nqk + 2 * C_MV]

    graw = p_ref[:, GATE_OFF:GATE_OFF + LANES] + gb_ref[...]
    capped = GATE_CAP * jnp.tanh(graw / GATE_CAP)
    logsig = jnp.minimum(capped, 0.0) - jnp.log(1.0 + jnp.exp(-jnp.abs(capped)))
    lane_g = lax.broadcasted_iota(jnp.int32, (tb, LANES), 1)
    gates = jnp.where(lane_g < H_M, capped, logsig)

    ig_sh = pltpu.roll(gates, H_M, axis=1)
    yield

    n2 = 2 * CHUNK
    tri = (lax.broadcasted_iota(jnp.int32, (CHUNK, CHUNK), 0)
           >= lax.broadcasted_iota(jnp.int32, (CHUNK, CHUNK), 1)).astype(BF16)
    lane_lo = lax.broadcasted_iota(jnp.int32, (CHUNK, LANES), 1) < DK_M
    causal_st = ((lax.broadcasted_iota(jnp.int32, (n2, CHUNK), 0) & (CHUNK - 1))
                 >= lax.broadcasted_iota(jnp.int32, (n2, CHUNK), 1))
    zpad = jnp.zeros((CHUNK, LANES), F32)
    ones_v = jnp.ones((CHUNK, DV_M), BF16)
    ones_sq = jnp.ones((DV_M, DV_M), BF16)
    row_sq = lax.broadcasted_iota(jnp.int32, (n2, LANES), 0)
    lane_sq = lax.broadcasted_iota(jnp.int32, (n2, LANES), 1)
    row_lo = row_sq < CHUNK
    n_chunks = tb // CHUNK
    n_pairs = H_M // 2
    gate_lane = [lane_sq == jnp.where(row_lo, H_M + 2 * pr, H_M + 2 * pr + 1) for pr in range(n_pairs)]
    spread = [(row_sq == jnp.where(lane_sq < DK_M, H_M + 2 * pr, H_M + 2 * pr + 1)).astype(BF16)
              for pr in range(n_pairs)]

    def stack2(x):
        return jnp.concatenate([jnp.where(lane_lo, x, 0.0), jnp.where(lane_lo, 0.0, x)], axis=0)

    def rep2(x):
        return jnp.concatenate([x, x], axis=0)

    def rows2(x0, x1, shape):
        return jnp.concatenate([jnp.broadcast_to(x0, shape), jnp.broadcast_to(x1, shape)], axis=0)

    prob = []
    tn = (((0,), (0,)), ((), ()))
    for c in range(n_chunks):
        sl = slice(c * CHUNK, (c + 1) * CHUNK)
        hi, lo = _split_hi_lo(gates[sl])
        cs = (jnp.dot(tri, hi, preferred_element_type=F32)
              + jnp.dot(tri, lo, preferred_element_type=F32))
        c_all = ig_sh[sl] - cs
        w_all = jnp.exp(c_all - jnp.max(c_all, axis=0, keepdims=True)).astype(BF16)
        c_t = jnp.concatenate([c_all, zpad], axis=0).T
        cs2 = rep2(cs)
        for pr in range(n_pairs):
            prob.append(dict(c=c, pr=pr, sl=sl, w_all=w_all, c_t=c_t, cs2=cs2))
    yield
    for q in prob:
        ls = slice(q["pr"] * LANES, (q["pr"] + 1) * LANES)
        q["q_st"] = stack2(q_all[q["sl"], ls]).astype(BF16)
        q["k_sl"] = k_all[q["sl"], ls]
        q["s_qk"] = lax.dot_general(q["q_st"], q["k_sl"].astype(BF16), (((1,), (1,)), ((), ())),
                                    preferred_element_type=F32)
    yield
    for q in prob:
        h0, h1 = 2 * q["pr"], 2 * q["pr"] + 1
        q["b_col"] = jnp.sum(jnp.where(gate_lane[q["pr"]], q["cs2"], 0.0), axis=1, keepdims=True)
        c_row = rows2(q["c_t"][H_M + h0:H_M + h0 + 1, 0:CHUNK], q["c_t"][H_M + h1:H_M + h1 + 1, 0:CHUNK],
                      (CHUNK, CHUNK))
        q["masked"] = jnp.where(causal_st, c_row, -jnp.inf)
        q["cm"] = jnp.max(q["masked"], axis=1, keepdims=True)
    yield
    for q in prob:
        h0, h1 = 2 * q["pr"], 2 * q["pr"] + 1
        q["s_d"] = (q["s_qk"] * jnp.exp(q["masked"] - q["cm"])).astype(BF16)
        q["v0"] = jnp.concatenate([v_all[q["sl"], h0 * DV_M:(h0 + 1) * DV_M].astype(BF16), ones_v], axis=1)
        q["v1"] = jnp.concatenate([v_all[q["sl"], h1 * DV_M:(h1 + 1) * DV_M].astype(BF16), ones_v], axis=1)
        q["kw"] = stack2(q["k_sl"] * jnp.dot(q["w_all"], spread[q["pr"]],
                                             preferred_element_type=F32)).astype(BF16)
    yield
    for q in prob:
        q["sv"] = jnp.concatenate([jnp.dot(q["s_d"][0:CHUNK], q["v0"], preferred_element_type=F32),
                                   jnp.dot(q["s_d"][CHUNK:n2], q["v1"], preferred_element_type=F32)], axis=0)
        q["kv"] = lax.dot_general(q["kw"], jnp.concatenate([q["v0"], q["v1"]], axis=0), tn,
                                  preferred_element_type=F32)
        q["b_last"] = rows2(q["b_col"][CHUNK - 1:CHUNK], q["b_col"][n2 - 1:n2], (CHUNK, 1))
        q["a_max"] = q["b_last"] + rows2(q["cm"][CHUNK - 1:CHUNK], q["cm"][n2 - 1:n2], (CHUNK, 1))
    yield

    cn_st = [cn_ref[i] for i in range(n_pairs)]
    m_st = [m_ref[i] for i in range(n_pairs)]
    for q in prob:
        pr = q["pr"]
        q["m_prev"] = m_st[pr]
        q["cn_prev"] = cn_st[pr].astype(BF16)
        m_new = jnp.maximum(q["b_last"] + q["m_prev"], q["a_max"])
        dec = jnp.exp(q["b_last"] + q["m_prev"] - m_new)
        beta = jnp.exp(q["a_max"] - m_new)
        cn_st[pr] = dec * cn_st[pr] + beta * q["kv"]
        m_st[pr] = m_new
    for pr in range(n_pairs):
        cn_ref[pr] = cn_st[pr]
        m_ref[pr] = m_st[pr]
    yield
    for q in prob:
        q["mx"] = jnp.maximum(q["cm"], q["m_prev"])
        q["qc"] = jnp.dot(q["q_st"], q["cn_prev"], preferred_element_type=F32)
    yield
    for q in prob:
        mix = jnp.exp(q["cm"] - q["mx"]) * q["sv"] + jnp.exp(q["m_prev"] - q["mx"]) * q["qc"]
        num = mix[:, 0:DV_M]
        den = mix[:, DV_M:2 * DV_M]
        q["hout"] = num / jnp.maximum(jnp.abs(den), jnp.exp(-(q["b_col"] + q["mx"])))
    yield
    for q in prob:
        q["ms"] = jnp.dot((q["hout"] * q["hout"]).astype(BF16), ones_sq,
                          preferred_element_type=F32) * (1.0 / DV_M)
    yield
    y_rows = []
    for c in range(n_chunks):
        y_heads = []
        for pr in range(n_pairs):
            q = prob[c * n_pairs + pr]
            hn = q["hout"] * lax.rsqrt(q["ms"] + NORM_EPS)
            y_heads.append(hn[0:CHUNK])
            y_heads.append(hn[CHUNK:n2])
        y_rows.append(jnp.concatenate(y_heads, axis=1))
    y = jnp.concatenate(y_rows, axis=0)
    o_ref[...] = (y * ng_ref[...] * _sigmoid(o_gate)).astype(o_ref.dtype)
    yield


RWKV_LEAD_STAGES = 7
RWKV_STAGES_PER_MLSTM = 2


def _timemix_kernel(prw_ref, vec_ref, wd_ref, wa_ref, wg_ref, pml_ref, gb_ref, ng_ref,
                    orw_ref, oml_ref, rstate_ref, cn_ref, m_ref, *, tb):
    rw = _rwkv_stages(prw_ref, vec_ref, wd_ref, wa_ref, wg_ref, orw_ref, rstate_ref, tb)
    ml = _mlstm_stages(pml_ref, gb_ref, ng_ref, oml_ref, cn_ref, m_ref, tb)
    for _ in range(RWKV_LEAD_STAGES):
        next(rw)
    order = [rw] * RWKV_STAGES_PER_MLSTM + [ml]
    done = set()
    while len(done) < 2:
        for gen in order:
            if id(gen) not in done and next(gen, StopIteration) is StopIteration:
                done.add(id(gen))


def _timemix(p_rw, vecs, wd, wa, wg, p_ml, gb, ng, tb):
    bsz, t, _ = p_rw.shape
    const = lambda shape: pl.BlockSpec(shape, lambda b, i: (0,) * len(shape))
    blk = lambda width: pl.BlockSpec((None, tb, width), lambda b, i: (b, i, 0))
    return pl.pallas_call(
        functools.partial(_timemix_kernel, tb=tb),
        grid=(bsz, t // tb),
        in_specs=[
            blk(RW_COLS),
            const((SUBLANES, C_R)),
            const((LANES, C_R)),
            const((LANES, C_R)),
            const((GATE_LORA, C_R)),
            blk(ML_PAD),
            const((1, LANES)),
            const((1, C_MV)),
        ],
        out_specs=[blk(C_R), blk(C_MV)],
        out_shape=[jax.ShapeDtypeStruct((bsz, t, C_R), BF16),
                   jax.ShapeDtypeStruct((bsz, t, C_MV), BF16)],
        scratch_shapes=[
            pltpu.VMEM((H_R // 2, 2 * N_R, 2 * N_R), F32),
            pltpu.VMEM((H_M // 2, 2 * DK_M, 2 * DV_M), F32),
            pltpu.VMEM((H_M // 2, 2 * CHUNK, 1), F32),
        ],
        compiler_params=pltpu.CompilerParams(
            dimension_semantics=("parallel", "arbitrary"), vmem_limit_bytes=VMEM_LIMIT),
        name="timemix",
    )(p_rw, vecs, wd, wa, wg, p_ml, gb, ng)


def _ffn_kernel(x_ref, yr_ref, ym_ref, wo_ref, g2_ref, wup_ref, cw_ref, cb_ref, wdn_ref, gf_ref,
                o_ref, x1_ref, h2_ref, act_ref, carry_ref, *, tm, tiles_per_seq):
    i = pl.program_id(0)

    @pl.when(i % tiles_per_seq == 0)
    def _():
        carry_ref[...] = jnp.zeros_like(carry_ref)

    x1 = (x_ref[...]
          + jnp.dot(yr_ref[...], wo_ref[0:C_R, :], preferred_element_type=F32)
          + jnp.dot(ym_ref[...], wo_ref[C_R:C_R + C_MV, :], preferred_element_type=F32))
    x1_ref[...] = x1
    h2_ref[...] = _rmsnorm_rows(x1, g2_ref[...]).astype(BF16)

    fc = 256
    for j in range(D_FF // fc):
        cs = slice(j * fc, (j + 1) * fc)
        h2 = h2_ref[...]
        a = jnp.dot(h2, wup_ref[:, j * fc:(j + 1) * fc], preferred_element_type=F32)
        b = jnp.dot(h2, wup_ref[:, D_FF + j * fc:D_FF + (j + 1) * fc], preferred_element_type=F32)
        ext = jnp.concatenate([carry_ref[:, cs], a], axis=0)
        carry_ref[:, cs] = a[tm - SUBLANES:tm, :]
        conv = cb_ref[:, cs] + a * cw_ref[FFN_CONV - 1:FFN_CONV, cs]
        for s in range(1, FFN_CONV):
            sh = pltpu.roll(ext, s, axis=0)[SUBLANES:SUBLANES + tm]
            conv = conv + sh * cw_ref[FFN_CONV - 1 - s:FFN_CONV - s, cs]
        act_ref[:, cs] = (conv * _sigmoid(conv) * b).astype(BF16)

    nc = 256
    for j in range(D_MODEL // nc):
        cs = slice(j * nc, (j + 1) * nc)
        x1_ref[:, cs] = x1_ref[:, cs] + jnp.dot(act_ref[...], wdn_ref[:, cs],
                                                preferred_element_type=F32)
    o_ref[...] = _rmsnorm_rows(x1_ref[...], gf_ref[...])


def _ffn(x2d, y_rw, y_ml, w_out, g2, w_up, cw, cb, w_dn, gf, tm, seq):
    n = x2d.shape[0]
    const = lambda shape: pl.BlockSpec(shape, lambda i: (0,) * len(shape),
                                       pipeline_mode=pl.Buffered(1))
    return pl.pallas_call(
        functools.partial(_ffn_kernel, tm=tm, tiles_per_seq=seq // tm),
        grid=(n // tm,),
        in_specs=[
            pl.BlockSpec((tm, D_MODEL), lambda i: (i, 0)),
            pl.BlockSpec((tm, C_R), lambda i: (i, 0)),
            pl.BlockSpec((tm, C_MV), lambda i: (i, 0)),
            const((C_R + C_MV, D_MODEL)),
            const((1, D_MODEL)),
            const((D_MODEL, 2 * D_FF)),
            const((FFN_CONV, D_FF)),
            const((1, D_FF)),
            const((D_FF, D_MODEL)),
            const((1, D_MODEL)),
        ],
        out_specs=pl.BlockSpec((tm, D_MODEL), lambda i: (i, 0)),
        out_shape=jax.ShapeDtypeStruct((n, D_MODEL), F32),
        scratch_shapes=[
            pltpu.VMEM((tm, D_MODEL), F32),
            pltpu.VMEM((tm, D_MODEL), BF16),
            pltpu.VMEM((tm, D_FF), BF16),
            pltpu.VMEM((SUBLANES, D_FF), F32),
        ],
        compiler_params=pltpu.CompilerParams(
            dimension_semantics=("arbitrary",), vmem_limit_bytes=VMEM_LIMIT),
        name="outproj_convffn",
    )(x2d, y_rw, y_ml, w_out, g2, w_up, cw, cb, w_dn, gf)


def _row(v):
    return v.reshape(1, -1)


def _layer(x, norm1_g, w_in, rw_mu, w0, w_up_decay, a0, w_up_a, w_up_g, k_k, k_a, r_k,
           lnx_w, lnx_b, qk_conv_w, qk_conv_b, i_bias, f_bias, mh_norm_g, w_out,
           norm2_g, w_ffn_up, ffn_conv_w, ffn_conv_b, w_ffn_down, out_g):
    bsz, seq, _ = x.shape
    n = bsz * seq
    tm = min(1024, seq)
    tb = min(256, seq)
    x2d = x.reshape(n, D_MODEL)

    w_in_p = jnp.pad(w_in, ((0, 0), (0, ML_PAD - ML_COLS))).astype(BF16)
    p_rw, p_ml = _inproj(x2d, _row(norm1_g), w_in_p, _row(rw_mu), qk_conv_w, _row(qk_conv_b), tm, seq)

    zeros_lora = jnp.zeros((DECAY_LORA, C_R), F32)
    wd = jnp.concatenate([w_up_decay, zeros_lora], axis=0).astype(BF16)
    wa = jnp.concatenate([zeros_lora, w_up_a], axis=0).astype(BF16)
    vecs = jnp.stack([w0, a0, k_k, k_a, r_k.reshape(C_R), lnx_w, lnx_b, jnp.zeros_like(w0)], axis=0)
    gate_bias = jnp.pad(jnp.concatenate([i_bias, f_bias]), (0, LANES - 2 * H_M))
    y_rw, y_ml = _timemix(p_rw.reshape(bsz, seq, RW_COLS), vecs, wd, wa, w_up_g.astype(BF16),
                          p_ml.reshape(bsz, seq, ML_PAD), _row(gate_bias), _row(mh_norm_g), tb)

    out = _ffn(x2d, y_rw.reshape(n, C_R), y_ml.reshape(n, C_MV), w_out.astype(BF16), _row(norm2_g),
               w_ffn_up.astype(BF16), ffn_conv_w, _row(ffn_conv_b), w_ffn_down.astype(BF16),
               _row(out_g), tm, seq)
    return out.reshape(bsz, seq, D_MODEL)


def kernel(x, norm1_g, w_in, rw_mu, w0, w_up_decay, a0, w_up_a, w_up_g, k_k, k_a, r_k, lnx_w, lnx_b,
           qk_conv_w, qk_conv_b, i_bias, f_bias, mh_norm_g, w_out, norm2_g, w_ffn_up, ffn_conv_w,
           ffn_conv_b, w_ffn_down, norm_f_g):
    assert norm1_g.shape[0] == 1
    return _layer(x, norm1_g[0], w_in[0], rw_mu[0], w0[0], w_up_decay[0], a0[0], w_up_a[0],
                  w_up_g[0], k_k[0], k_a[0], r_k[0], lnx_w[0], lnx_b[0], qk_conv_w[0],
                  qk_conv_b[0], i_bias[0], f_bias[0], mh_norm_g[0], w_out[0], norm2_g[0],
                  w_ffn_up[0], ffn_conv_w[0], ffn_conv_b[0], w_ffn_down[0], norm_f_g)
```

```python
import functools
import math

import jax
import jax.numpy as jnp
from jax import lax
from jax.experimental import pallas as pl
from jax.experimental.pallas import tpu as pltpu

F32 = jnp.float32
BF16 = jnp.bfloat16

D_MODEL = 1024
H_R, N_R = 8, 64
C_R = H_R * N_R
DECAY_LORA, AAA_LORA, GATE_LORA = 64, 64, 128
RW_COLS = 3 * C_R + DECAY_LORA + AAA_LORA + GATE_LORA
LNX_EPS = 64e-5
H_M, DK_M, DV_M = 4, 64, 128
C_MQK = H_M * DK_M
C_MV = H_M * DV_M
ML_COLS = 2 * C_MQK + 2 * C_MV + 2 * H_M
QK_CONV = 4
GATE_CAP = 15.0
D_FF = 2816
FFN_CONV = 3
NORM_EPS = 1e-6
CHUNK = 64

LANES = 128
SUBLANES = 8
BF16_ROWS = 16
ML_PAD = 1664
GATE_OFF = 2 * C_MQK + 2 * C_MV
VMEM_LIMIT = 56 * 1024 * 1024

EXP_M05 = math.exp(-0.5)


def _dot(a, b):
    return jnp.dot(a.astype(BF16), b.astype(BF16), preferred_element_type=F32)


def _split_hi_lo(x):
    hi = x.astype(BF16)
    lo = (x - hi.astype(F32)).astype(BF16)
    return hi, lo


def _sigmoid(x):
    return 1.0 / (1.0 + jnp.exp(-x))


def _rmsnorm_rows(x, g):
    ms = jnp.mean(x * x, axis=-1, keepdims=True)
    return x * lax.rsqrt(ms + NORM_EPS) * g


def _inproj_kernel(x_ref, g_ref, w_ref, mu_ref, cw_ref, cb_ref, orw_ref, oml_ref,
                   rcarry_ref, mcarry_ref, *, tm, tiles_per_seq):
    i = pl.program_id(0)

    @pl.when(i % tiles_per_seq == 0)
    def _():
        rcarry_ref[...] = jnp.zeros_like(rcarry_ref)
        mcarry_ref[...] = jnp.zeros_like(mcarry_ref)

    h = _rmsnorm_rows(x_ref[...], g_ref[...]).astype(BF16)
    row0 = lax.broadcasted_iota(jnp.int32, (tm, 1), 0) == 0
    step = 256
    for c in range(0, RW_COLS, step):
        cs = slice(c, c + step)
        p = jnp.dot(h, w_ref[:, cs], preferred_element_type=F32)
        shifted = jnp.where(row0, rcarry_ref[SUBLANES - 1:SUBLANES, cs], pltpu.roll(p, 1, axis=0))
        rcarry_ref[:, cs] = p[tm - SUBLANES:tm, :]
        orw_ref[:, cs] = p + (shifted - p) * mu_ref[:, cs]
    nqk = 2 * C_MQK
    for c in range(0, nqk, step):
        cs = slice(c, c + step)
        p = jnp.dot(h, w_ref[:, RW_COLS + c:RW_COLS + c + step], preferred_element_type=F32)
        ext = jnp.concatenate([mcarry_ref[:, cs], p], axis=0)
        mcarry_ref[:, cs] = p[tm - SUBLANES:tm, :]
        acc = cb_ref[:, cs] + p * cw_ref[QK_CONV - 1:QK_CONV, cs]
        for s in range(1, QK_CONV):
            sh = pltpu.roll(ext, s, axis=0)[SUBLANES:SUBLANES + tm]
            acc = acc + sh * cw_ref[QK_CONV - 1 - s:QK_CONV - s, cs]
        qk = acc * _sigmoid(acc)
        if c < C_MQK:
            qk = qk * (DK_M ** -0.5)
        oml_ref[:, cs] = qk
    for c in range(nqk, ML_PAD, step):
        n = min(step, ML_PAD - c)
        oml_ref[:, c:c + n] = jnp.dot(h, w_ref[:, RW_COLS + c:RW_COLS + c + n],
                                      preferred_element_type=F32)


def _inproj(x2d, g1, w_in_p, mu, cw, cb, tm, seq):
    n = x2d.shape[0]
    const = lambda shape: pl.BlockSpec(shape, lambda i: (0,) * len(shape))
    return pl.pallas_call(
        functools.partial(_inproj_kernel, tm=tm, tiles_per_seq=seq // tm),
        grid=(n // tm,),
        in_specs=[
            pl.BlockSpec((tm, D_MODEL), lambda i: (i, 0)),
            const((1, D_MODEL)),
            const((D_MODEL, RW_COLS + ML_PAD)),
            const((1, RW_COLS)),
            const((QK_CONV, 2 * C_MQK)),
            const((1, 2 * C_MQK)),
        ],
        out_specs=[
            pl.BlockSpec((tm, RW_COLS), lambda i: (i, 0)),
            pl.BlockSpec((tm, ML_PAD), lambda i: (i, 0)),
        ],
        out_shape=[
            jax.ShapeDtypeStruct((n, RW_COLS), F32),
            jax.ShapeDtypeStruct((n, ML_PAD), F32),
        ],
        scratch_shapes=[
            pltpu.VMEM((SUBLANES, RW_COLS), F32),
            pltpu.VMEM((SUBLANES, 2 * C_MQK), F32),
        ],
        compiler_params=pltpu.CompilerParams(
            dimension_semantics=("arbitrary",), vmem_limit_bytes=VMEM_LIMIT),
        name="inproj",
    )(x2d, g1, w_in_p, mu, cw, cb)


def _rwkv_stages(p_ref, vec_ref, wd_ref, wa_ref, wg_ref, o_ref, state_ref, tb):
    t_idx = pl.program_id(1)

    @pl.when(t_idx == 0)
    def _():
        state_ref[...] = jnp.zeros_like(state_ref)

    w0 = vec_ref[0:1, :]
    a0 = vec_ref[1:2, :]
    k_k = vec_ref[2:3, :]
    k_a = vec_ref[3:4, :]
    r_k = vec_ref[4:5, :]
    lnx_w = vec_ref[5:6, :]
    lnx_b = vec_ref[6:7, :]

    r = p_ref[:, 0:C_R]
    k = p_ref[:, C_R:2 * C_R]
    v = p_ref[:, 2 * C_R:3 * C_R]
    z = p_ref[:, 3 * C_R:3 * C_R + LANES]
    g_lo = p_ref[:, 3 * C_R + LANES:RW_COLS]

    u = w0 + _dot(jnp.tanh(z), wd_ref[...])
    ld = -EXP_M05 * _sigmoid(u)
    a = _sigmoid(a0 + _dot(z, wa_ref[...]))
    g = _dot(_sigmoid(g_lo), wg_ref[...])
    kk = k * k_k
    kmod = k * (1.0 + (a - 1.0) * k_a)
    rkr = r * kmod * r_k
    yield

    n2 = 2 * CHUNK
    ri = lax.broadcasted_iota(jnp.int32, (n2, n2), 0)
    ci = lax.broadcasted_iota(jnp.int32, (n2, n2), 1)
    same_head = (ri < CHUNK) == (ci < N_R)
    bd_sel = same_head.astype(F32).astype(BF16)
    ti = lax.broadcasted_iota(jnp.int32, (CHUNK, LANES), 0)
    si = lax.broadcasted_iota(jnp.int32, (CHUNK, LANES), 1) & (CHUNK - 1)
    strict_lower = ti > si
    lower = ti >= si
    eye = (ti == si).astype(F32).astype(BF16)
    lvl_fl, lvl_bd = [], []
    for l in range(6):
        m = (((ti >> (l + 1)) == (si >> (l + 1))) & (((ti >> l) & 1) == 1) & (((si >> l) & 1) == 0))
        lvl_fl.append(m.astype(F32).astype(BF16))
        rs_, cs_ = ri & (CHUNK - 1), ci & (CHUNK - 1)
        m = (same_head & ((rs_ >> (l + 1)) == (cs_ >> (l + 1)))
             & (((rs_ >> l) & 1) == 1) & (((cs_ >> l) & 1) == 0))
        lvl_bd.append(m.astype(F32).astype(BF16))
    lane_lo = lax.broadcasted_iota(jnp.int32, (CHUNK, LANES), 1) < N_R
    tri = (lax.broadcasted_iota(jnp.int32, (CHUNK, CHUNK), 0)
           >= lax.broadcasted_iota(jnp.int32, (CHUNK, CHUNK), 1)).astype(BF16)

    def bd(x):
        return jnp.concatenate([x, x], axis=0) * bd_sel

    def head_bcast(lo_col, hi_col):
        return jnp.where(lane_lo, lo_col, hi_col)

    def head_sums(x):
        return head_bcast(jnp.sum(jnp.where(lane_lo, x, 0.0), axis=1, keepdims=True),
                          jnp.sum(jnp.where(lane_lo, 0.0, x), axis=1, keepdims=True))

    n_chunks = tb // CHUNK
    n_pairs = H_R // 2
    zeros_fl = jnp.zeros((CHUNK, LANES), BF16)
    inv_n = 1.0 / N_R

    prob = []
    for c in range(n_chunks):
        sl = slice(c * CHUNK, (c + 1) * CHUNK)
        ld_c = ld[sl]
        hi, lo = _split_hi_lo(ld_c)
        cs = (jnp.dot(tri, hi, preferred_element_type=F32)
              + jnp.dot(tri, lo, preferred_element_type=F32))
        g_in = jnp.exp(cs)
        g_ex = jnp.exp(cs - ld_c)
        g_inv = jnp.exp(-cs)
        rt = r[sl] * g_in
        kt = kmod[sl] * g_inv
        ag = a[sl] * g_inv
        g_last = g_in[CHUNK - 1:CHUNK, :]
        for pr in range(n_pairs):
            ls = slice(pr * LANES, (pr + 1) * LANES)
            kk_p = kk[sl, ls]
            kkn = kk_p * jnp.minimum(lax.rsqrt(head_sums(kk_p * kk_p)), 1e12)
            a_fl = (-kkn * g_ex[:, ls]).astype(BF16)
            b_fl = (kkn * ag[:, ls]).astype(BF16)
            r_fl = rt[:, ls].astype(BF16)
            k_fl = kt[:, ls].astype(BF16)
            v_fl = v[sl, ls].astype(BF16)
            rkr_p = rkr[sl, ls]
            bonus = head_bcast(jnp.sum(jnp.where(lane_lo, rkr_p, 0.0), axis=1, keepdims=True),
                               jnp.sum(jnp.where(lane_lo, 0.0, rkr_p), axis=1, keepdims=True)) * v[sl, ls]
            gram = lax.dot_general(jnp.concatenate([a_fl, r_fl], axis=0),
                                   jnp.concatenate([bd(b_fl), bd(k_fl)], axis=0),
                                   (((1,), (1,)), ((), ())), preferred_element_type=F32)
            g_ab = gram[0:CHUNK, 0:n2].astype(BF16)
            x_fl = eye + g_ab * lvl_fl[0]
            gl_col = jnp.broadcast_to(g_last[:, ls], (n2, n2)).T
            prob.append(dict(
                a_bd=bd(a_fl), r_fl=r_fl, bk_fl=jnp.concatenate([b_fl, k_fl], axis=0), v_fl=v_fl, v_bd=bd(v_fl),
                g_bd=bd(g_ab), bonus=bonus,
                a_ak=jnp.where(strict_lower, gram[0:CHUNK, n2:2 * n2], 0.0).astype(BF16),
                a_rb=jnp.where(lower, gram[CHUNK:n2, 0:n2], 0.0).astype(BF16),
                a_rk=jnp.where(lower, gram[CHUNK:n2, n2:2 * n2], 0.0).astype(BF16),
                x=x_fl, x_bd=bd(x_fl), g_last=gl_col))
        yield
    def odd_blocks(x, s):
        return jnp.concatenate([x[i * s:(i + 1) * s] for i in range(1, CHUNK // s, 2)], axis=0)

    def add_odd_blocks(x, u, s):
        parts = []
        for i in range(CHUNK // s):
            blk = x[i * s:(i + 1) * s]
            parts.append(blk + u[(i // 2) * s:(i // 2 + 1) * s] if i % 2 else blk)
        return jnp.concatenate(parts, axis=0)

    for l in range(1, 6):
        s = 1 << l
        sub = s >= BF16_ROWS
        for q in prob:
            lhs = odd_blocks(q["x"], s) if sub else q["x"]
            q["t"] = jnp.dot(lhs, q["g_bd"] * lvl_bd[l], preferred_element_type=F32).astype(BF16)
        yield
        for q in prob:
            u = jnp.dot(q["t"], q["x_bd"], preferred_element_type=F32).astype(BF16)
            q["x"] = add_odd_blocks(q["x"], u, s) if sub else q["x"] + u
            q["x_bd"] = bd(q["x"])
        yield
    for q in prob:
        s1 = jnp.dot(jnp.concatenate([q["a_ak"], q["a_rk"]], axis=0), q["v_bd"],
                     preferred_element_type=F32)
        q["wv"] = s1[0:CHUNK].astype(BF16)
        q["arkv"] = s1[CHUNK:n2]
    yield
    for q in prob:
        q["xc"] = jnp.dot(q["x"], jnp.concatenate([bd(q["wv"]), q["a_bd"]], axis=1),
                          preferred_element_type=F32).astype(BF16)
    yield
    for q in prob:
        xwv, xpa = q["xc"][:, 0:n2], q["xc"][:, n2:2 * n2]
        s2 = jnp.dot(q["a_rb"], jnp.concatenate([bd(xwv), bd(xpa)], axis=1),
                     preferred_element_type=F32)
        q["yv"] = s2[:, 0:n2] + q["arkv"]
        r_eff = q["r_fl"].astype(F32) + s2[:, n2:2 * n2]
        rhs = jnp.concatenate([q["xc"], jnp.concatenate([q["v_fl"], zeros_fl], axis=1)], axis=0)
        hb = lax.dot_general(q["bk_fl"], rhs, (((0,), (0,)), ((), ())),
                             preferred_element_type=F32)
        q["h"] = jnp.where(same_head, hb[:, 0:n2], 0.0)
        q["lhs"] = jnp.concatenate([r_eff.astype(BF16),
                                    jnp.where(same_head, hb[:, n2:2 * n2], 0.0).astype(BF16)], axis=0)
    yield

    states = [state_ref[i] for i in range(n_pairs)]
    for c in range(n_chunks):
        for pr in range(n_pairs):
            q = prob[c * n_pairs + pr]
            m0 = states[pr]
            o2 = jnp.dot(q["lhs"], m0.astype(BF16), preferred_element_type=F32)
            states[pr] = q["g_last"] * (m0 + o2[CHUNK:CHUNK + n2] + q["h"])
            q["y"] = o2[0:CHUNK] + q["yv"]
        yield
    for pr in range(n_pairs):
        state_ref[pr] = states[pr]
    for q in prob:
        q["mean"] = head_sums(q["y"]) * inv_n
    yield
    for q in prob:
        q["d"] = q["y"] - q["mean"]
        q["var"] = head_sums(q["d"] * q["d"]) * inv_n
    yield
    for c in range(n_chunks):
        sl = slice(c * CHUNK, (c + 1) * CHUNK)
        for pr in range(n_pairs):
            ls = slice(pr * LANES, (pr + 1) * LANES)
            q = prob[c * n_pairs + pr]
            y = q["d"] * lax.rsqrt(q["var"] + LNX_EPS) * lnx_w[:, ls] + lnx_b[:, ls] + q["bonus"]
            o_ref[sl, ls] = (y * g[sl, ls]).astype(o_ref.dtype)
    yield


def _mlstm_stages(p_ref, gb_ref, ng_ref, o_ref, cn_ref, m_ref, tb):
    t_idx = pl.program_id(1)

    @pl.when(t_idx == 0)
    def _():
        cn_ref[...] = jnp.zeros_like(cn_ref)
        m_ref[...] = jnp.zeros_like(m_ref)

    nqk = 2 * C_MQK
    q_all = p_ref[:, 0:C_MQK]
    k_all = p_ref[:, C_MQK:nqk]
    v_all = p_ref[:, nqk:nqk + C_MV]
    o_gate = p_ref[:, nqk + C_MV:nqk + 2 * C_MV]

    graw = p_ref[:, GATE_OFF:GATE_OFF + LANES] + gb_ref[...]
    capped = GATE_CAP * jnp.tanh(graw / GATE_CAP)
    logsig = jnp.minimum(capped, 0.0) - jnp.log(1.0 + jnp.exp(-jnp.abs(capped)))
    lane_g = lax.broadcasted_iota(jnp.int32, (tb, LANES), 1)
    gates = jnp.where(lane_g < H_M, capped, logsig)

    ig_sh = pltpu.roll(gates, H_M, axis=1)
    yield

    n2 = 2 * CHUNK
    tri = (lax.broadcasted_iota(jnp.int32, (CHUNK, CHUNK), 0)
           >= lax.broadcasted_iota(jnp.int32, (CHUNK, CHUNK), 1)).astype(BF16)
    lane_lo = lax.broadcasted_iota(jnp.int32, (CHUNK, LANES), 1) < DK_M
    causal_st = ((lax.broadcasted_iota(jnp.int32, (n2, CHUNK), 0) & (CHUNK - 1))
                 >= lax.broadcasted_iota(jnp.int32, (n2, CHUNK), 1))
    zpad = jnp.zeros((CHUNK, LANES), F32)
    ones_v = jnp.ones((CHUNK, DV_M), BF16)
    ones_sq = jnp.ones((DV_M, DV_M), BF16)
    row_sq = lax.broadcasted_iota(jnp.int32, (n2, LANES), 0)
    lane_sq = lax.broadcasted_iota(jnp.int32, (n2, LANES), 1)
    row_lo = row_sq < CHUNK
    n_chunks = tb // CHUNK
    n_pairs = H_M // 2
    gate_lane = [lane_sq == jnp.where(row_lo, H_M + 2 * pr, H_M + 2 * pr + 1) for pr in range(n_pairs)]
    spread = [(row_sq == jnp.where(lane_sq < DK_M, H_M + 2 * pr, H_M + 2 * pr + 1)).astype(BF16)
              for pr in range(n_pairs)]

    def stack2(x):
        return jnp.concatenate([jnp.where(lane_lo, x, 0.0), jnp.where(lane_lo, 0.0, x)], axis=0)

    def rep2(x):
        return jnp.concatenate([x, x], axis=0)

    def rows2(x0, x1, shape):
        return jnp.concatenate([jnp.broadcast_to(x0, shape), jnp.broadcast_to(x1, shape)], axis=0)

    prob = []
    tn = (((0,), (0,)), ((), ()))
    for c in range(n_chunks):
        sl = slice(c * CHUNK, (c + 1) * CHUNK)
        hi, lo = _split_hi_lo(gates[sl])
        cs = (jnp.dot(tri, hi, preferred_element_type=F32)
              + jnp.dot(tri, lo, preferred_element_type=F32))
        c_all = ig_sh[sl] - cs
        w_all = jnp.exp(c_all - jnp.max(c_all, axis=0, keepdims=True)).astype(BF16)
        c_t = jnp.concatenate([c_all, zpad], axis=0).T
        cs2 = rep2(cs)
        for pr in range(n_pairs):
            prob.append(dict(c=c, pr=pr, sl=sl, w_all=w_all, c_t=c_t, cs2=cs2))
    yield
    for q in prob:
        ls = slice(q["pr"] * LANES, (q["pr"] + 1) * LANES)
        q["q_st"] = stack2(q_all[q["sl"], ls]).astype(BF16)
        q["k_sl"] = k_all[q["sl"], ls]
        q["s_qk"] = lax.dot_general(q["q_st"], q["k_sl"].astype(BF16), (((1,), (1,)), ((), ())),
                                    preferred_element_type=F32)
    yield
    for q in prob:
        h0, h1 = 2 * q["pr"], 2 * q["pr"] + 1
        q["b_col"] = jnp.sum(jnp.where(gate_lane[q["pr"]], q["cs2"], 0.0), axis=1, keepdims=True)
        c_row = rows2(q["c_t"][H_M + h0:H_M + h0 + 1, 0:CHUNK], q["c_t"][H_M + h1:H_M + h1 + 1, 0:CHUNK],
                      (CHUNK, CHUNK))
        q["masked"] = jnp.where(causal_st, c_row, -jnp.inf)
        q["cm"] = jnp.max(q["masked"], axis=1, keepdims=True)
    yield
    for q in prob:
        h0, h1 = 2 * q["pr"], 2 * q["pr"] + 1
        q["s_d"] = (q["s_qk"] * jnp.exp(q["masked"] - q["cm"])).astype(BF16)
        q["v0"] = jnp.concatenate([v_all[q["sl"], h0 * DV_M:(h0 + 1) * DV_M].astype(BF16), ones_v], axis=1)
        q["v1"] = jnp.concatenate([v_all[q["sl"], h1 * DV_M:(h1 + 1) * DV_M].astype(BF16), ones_v], axis=1)
        q["kw"] = stack2(q["k_sl"] * jnp.dot(q["w_all"], spread[q["pr"]],
                                             preferred_element_type=F32)).astype(BF16)
    yield
    for q in prob:
        q["sv"] = jnp.concatenate([jnp.dot(q["s_d"][0:CHUNK], q["v0"], preferred_element_type=F32),
                                   jnp.dot(q["s_d"][CHUNK:n2], q["v1"], preferred_element_type=F32)], axis=0)
        q["kv"] = lax.dot_general(q["kw"], jnp.concatenate([q["v0"], q["v1"]], axis=0), tn,
                                  preferred_element_type=F32)
        q["b_last"] = rows2(q["b_col"][CHUNK - 1:CHUNK], q["b_col"][n2 - 1:n2], (CHUNK, 1))
        q["a_max"] = q["b_last"] + rows2(q["cm"][CHUNK - 1:CHUNK], q["cm"][n2 - 1:n2], (CHUNK, 1))
    yield

    cn_st = [cn_ref[i] for i in range(n_pairs)]
    m_st = [m_ref[i] for i in range(n_pairs)]
    for q in prob:
        pr = q["pr"]
        q["m_prev"] = m_st[pr]
        q["cn_prev"] = cn_st[pr].astype(BF16)
        m_new = jnp.maximum(q["b_last"] + q["m_prev"], q["a_max"])
        dec = jnp.exp(q["b_last"] + q["m_prev"] - m_new)
        beta = jnp.exp(q["a_max"] - m_new)
        cn_st[pr] = dec * cn_st[pr] + beta * q["kv"]
        m_st[pr] = m_new
    for pr in range(n_pairs):
        cn_ref[pr] = cn_st[pr]
        m_ref[pr] = m_st[pr]
    yield
    for q in prob:
        q["mx"] = jnp.maximum(q["cm"], q["m_prev"])
        q["qc"] = jnp.dot(q["q_st"], q["cn_prev"], preferred_element_type=F32)
    yield
    for q in prob:
        mix = jnp.exp(q["cm"] - q["mx"]) * q["sv"] + jnp.exp(q["m_prev"] - q["mx"]) * q["qc"]
        num = mix[:, 0:DV_M]
        den = mix[:, DV_M:2 * DV_M]
        q["hout"] = num / jnp.maximum(jnp.abs(den), jnp.exp(-(q["b_col"] + q["mx"])))
    yield
    for q in prob:
        q["ms"] = jnp.dot((q["hout"] * q["hout"]).astype(BF16), ones_sq,
                          preferred_element_type=F32) * (1.0 / DV_M)
    yield
    y_rows = []
    for c in range(n_chunks):
        y_heads = []
        for pr in range(n_pairs):
            q = prob[c * n_pairs + pr]
            hn = q["hout"] * lax.rsqrt(q["ms"] + NORM_EPS)
            y_heads.append(hn[0:CHUNK])
            y_heads.append(hn[CHUNK:n2])
        y_rows.append(jnp.concatenate(y_heads, axis=1))
    y = jnp.concatenate(y_rows, axis=0)
    o_ref[...] = (y * ng_ref[...] * _sigmoid(o_gate)).astype(o_ref.dtype)
    yield


RWKV_LEAD_STAGES = 7
RWKV_STAGES_PER_MLSTM = 2


def _timemix_kernel(prw_ref, vec_ref, wd_ref, wa_ref, wg_ref, pml_ref, gb_ref, ng_ref,
                    orw_ref, oml_ref, rstate_ref, cn_ref, m_ref, *, tb):
    rw = _rwkv_stages(prw_ref, vec_ref, wd_ref, wa_ref, wg_ref, orw_ref, rstate_ref, tb)
    ml = _mlstm_stages(pml_ref, gb_ref, ng_ref, oml_ref, cn_ref, m_ref, tb)
    for _ in range(RWKV_LEAD_STAGES):
        next(rw)
    order = [rw] * RWKV_STAGES_PER_MLSTM + [ml]
    done = set()
    while len(done) < 2:
        for gen in order:
            if id(gen) not in done and next(gen, StopIteration) is StopIteration:
                done.add(id(gen))


def _timemix(p_rw, vecs, wd, wa, wg, p_ml, gb, ng, tb):
    bsz, t, _ = p_rw.shape
    const = lambda shape: pl.BlockSpec(shape, lambda b, i: (0,) * len(shape))
    blk = lambda width: pl.BlockSpec((None, tb, width), lambda b, i: (b, i, 0))
    return pl.pallas_call(
        functools.partial(_timemix_kernel, tb=tb),
        grid=(bsz, t // tb),
        in_specs=[
            blk(RW_COLS),
            const((SUBLANES, C_R)),
            const((LANES, C_R)),
            const((LANES, C_R)),
            const((GATE_LORA, C_R)),
            blk(ML_PAD),
            const((1, LANES)),
            const((1, C_MV)),
        ],
        out_specs=[blk(C_R), blk(C_MV)],
        out_shape=[jax.ShapeDtypeStruct((bsz, t, C_R), BF16),
                   jax.ShapeDtypeStruct((bsz, t, C_MV), BF16)],
        scratch_shapes=[
            pltpu.VMEM((H_R // 2, 2 * N_R, 2 * N_R), F32),
            pltpu.VMEM((H_M // 2, 2 * DK_M, 2 * DV_M), F32),
            pltpu.VMEM((H_M // 2, 2 * CHUNK, 1), F32),
        ],
        compiler_params=pltpu.CompilerParams(
            dimension_semantics=("parallel", "arbitrary"), vmem_limit_bytes=VMEM_LIMIT),
        name="timemix",
    )(p_rw, vecs, wd, wa, wg, p_ml, gb, ng)


def _ffn_kernel(x_ref, yr_ref, ym_ref, wo_ref, g2_ref, wup_ref, cw_ref, cb_ref, wdn_ref, gf_ref,
                o_ref, x1_ref, h2_ref, act_ref, carry_ref, x2_ref, *, tm, tiles_per_seq, n_tiles):
    i = pl.program_id(0)

    @pl.when(i == 0)
    def _():
        x2_ref[...] = jnp.zeros_like(x2_ref)

    @pl.when(i % tiles_per_seq == 0)
    def _():
        carry_ref[...] = jnp.zeros_like(carry_ref)

    def finish_previous_tile():
        o_ref[...] = _rmsnorm_rows(x2_ref[...], gf_ref[...])

    @pl.when(i < n_tiles)
    def _():
        finish_previous_tile()
        x1 = (x_ref[...]
              + jnp.dot(yr_ref[...], wo_ref[0:C_R, :], preferred_element_type=F32)
              + jnp.dot(ym_ref[...], wo_ref[C_R:C_R + C_MV, :], preferred_element_type=F32))
        x1_ref[...] = x1
        h2_ref[...] = _rmsnorm_rows(x1, g2_ref[...]).astype(BF16)

        fc = 256
        for j in range(D_FF // fc):
            cs = slice(j * fc, (j + 1) * fc)
            h2 = h2_ref[...]
            a = jnp.dot(h2, wup_ref[:, j * fc:(j + 1) * fc], preferred_element_type=F32)
            b = jnp.dot(h2, wup_ref[:, D_FF + j * fc:D_FF + (j + 1) * fc], preferred_element_type=F32)
            ext = jnp.concatenate([carry_ref[:, cs], a], axis=0)
            carry_ref[:, cs] = a[tm - SUBLANES:tm, :]
            conv = cb_ref[:, cs] + a * cw_ref[FFN_CONV - 1:FFN_CONV, cs]
            for s in range(1, FFN_CONV):
                sh = pltpu.roll(ext, s, axis=0)[SUBLANES:SUBLANES + tm]
                conv = conv + sh * cw_ref[FFN_CONV - 1 - s:FFN_CONV - s, cs]
            act_ref[:, cs] = (conv * _sigmoid(conv) * b).astype(BF16)

        nc = 256
        for j in range(D_MODEL // nc):
            cs = slice(j * nc, (j + 1) * nc)
            x2_ref[:, cs] = x1_ref[:, cs] + jnp.dot(act_ref[...], wdn_ref[:, cs],
                                                    preferred_element_type=F32)

    @pl.when(i == n_tiles)
    def _():
        finish_previous_tile()


def _ffn(x2d, y_rw, y_ml, w_out, g2, w_up, cw, cb, w_dn, gf, tm, seq):
    n = x2d.shape[0]
    n_tiles = n // tm
    const = lambda shape: pl.BlockSpec(shape, lambda i: (0,) * len(shape),
                                       pipeline_mode=pl.Buffered(1))
    cur = lambda width: pl.BlockSpec((tm, width), lambda i: (jnp.minimum(i, n_tiles - 1), 0))
    return pl.pallas_call(
        functools.partial(_ffn_kernel, tm=tm, tiles_per_seq=seq // tm, n_tiles=n_tiles),
        grid=(n_tiles + 1,),
        in_specs=[
            cur(D_MODEL),
            cur(C_R),
            cur(C_MV),
            const((C_R + C_MV, D_MODEL)),
            const((1, D_MODEL)),
            const((D_MODEL, 2 * D_FF)),
            const((FFN_CONV, D_FF)),
            const((1, D_FF)),
            const((D_FF, D_MODEL)),
            const((1, D_MODEL)),
        ],
        out_specs=pl.BlockSpec((tm, D_MODEL), lambda i: (jnp.maximum(i - 1, 0), 0)),
        out_shape=jax.ShapeDtypeStruct((n, D_MODEL), F32),
        scratch_shapes=[
            pltpu.VMEM((tm, D_MODEL), F32),
            pltpu.VMEM((tm, D_MODEL), BF16),
            pltpu.VMEM((tm, D_FF), BF16),
            pltpu.VMEM((SUBLANES, D_FF), F32),
            pltpu.VMEM((tm, D_MODEL), F32),
        ],
        compiler_params=pltpu.CompilerParams(
            dimension_semantics=("arbitrary",), vmem_limit_bytes=VMEM_LIMIT),
        name="outproj_convffn",
    )(x2d, y_rw, y_ml, w_out, g2, w_up, cw, cb, w_dn, gf)


def _row(v):
    return v.reshape(1, -1)


def _layer(x, norm1_g, w_in, rw_mu, w0, w_up_decay, a0, w_up_a, w_up_g, k_k, k_a, r_k,
           lnx_w, lnx_b, qk_conv_w, qk_conv_b, i_bias, f_bias, mh_norm_g, w_out,
           norm2_g, w_ffn_up, ffn_conv_w, ffn_conv_b, w_ffn_down, out_g):
    bsz, seq, _ = x.shape
    n = bsz * seq
    tm = min(1024, seq)
    tb = min(256, seq)
    x2d = x.reshape(n, D_MODEL)

    w_in_p = jnp.pad(w_in, ((0, 0), (0, ML_PAD - ML_COLS))).astype(BF16)
    p_rw, p_ml = _inproj(x2d, _row(norm1_g), w_in_p, _row(rw_mu), qk_conv_w, _row(qk_conv_b), tm, seq)

    zeros_lora = jnp.zeros((DECAY_LORA, C_R), F32)
    wd = jnp.concatenate([w_up_decay, zeros_lora], axis=0).astype(BF16)
    wa = jnp.concatenate([zeros_lora, w_up_a], axis=0).astype(BF16)
    vecs = jnp.stack([w0, a0, k_k, k_a, r_k.reshape(C_R), lnx_w, lnx_b, jnp.zeros_like(w0)], axis=0)
    gate_bias = jnp.pad(jnp.concatenate([i_bias, f_bias]), (0, LANES - 2 * H_M))
    y_rw, y_ml = _timemix(p_rw.reshape(bsz, seq, RW_COLS), vecs, wd, wa, w_up_g.astype(BF16),
                          p_ml.reshape(bsz, seq, ML_PAD), _row(gate_bias), _row(mh_norm_g), tb)

    out = _ffn(x2d, y_rw.reshape(n, C_R), y_ml.reshape(n, C_MV), w_out.astype(BF16), _row(norm2_g),
               w_ffn_up.astype(BF16), ffn_conv_w, _row(ffn_conv_b), w_ffn_down.astype(BF16),
               _row(out_g), tm, seq)
    return out.reshape(bsz, seq, D_MODEL)


def kernel(x, norm1_g, w_in, rw_mu, w0, w_up_decay, a0, w_up_a, w_up_g, k_k, k_a, r_k, lnx_w, lnx_b,
           qk_conv_w, qk_conv_b, i_bias, f_bias, mh_norm_g, w_out, norm2_g, w_ffn_up, ffn_conv_w,
           ffn_conv_b, w_ffn_down, norm_f_g):
    assert norm1_g.shape[0] == 1
    return _layer(x, norm1_g[0], w_in[0], rw_mu[0], w0[0], w_up_decay[0], a0[0], w_up_a[0],
                  w_up_g[0], k_k[0], k_a[0], r_k[0], lnx_w[0], lnx_b[0], qk_conv_w[0],
                  qk_conv_b[0], i_bias[0], f_bias[0], mh_norm_g[0], w_out[0], norm2_g[0],
                  w_ffn_up[0], ffn_conv_w[0], ffn_conv_b[0], w_ffn_down[0], norm_f_g)
```
